```python
import jax, jax.numpy as jnp
from jax import lax
import numpy as np

D_MODEL = 2048
BATCH = 2
SEQ = 8192
DEPTH = 1
DEC_BATCH = 32
DEC_SEQ = 1
PAST_LEN = 16384
PAGE_SIZE = 128

MIX_WIDTH = D_MODEL
LRU_WIDTH = MIX_WIDTH // 2
ATT_WIDTH = MIX_WIDTH - LRU_WIDTH
LRU_BLOCKS = 8
LRU_BLOCK = LRU_WIDTH // LRU_BLOCKS
CONV_WIDTH = 4
LRU_C = 8.0
ATT_HEADS = 8
HEAD_DIM = ATT_WIDTH // ATT_HEADS
QUERY_BLOCK = 128
SB_BIAS_INIT = -6.0
N_EXPERTS = 64
TOP_K = 8
N_GROUPS = 8
TOPK_GROUPS = 4
EXPERT_DIM = 512
SHARED_DIM = 512
ROUTED_SCALE = 2.5
MOE_CHUNK = 512
PLE_DIM = 256
EPS = 1e-6
IN_WIDTH = 2 * LRU_WIDTH + 3 * ATT_WIDTH

kernel_name = "hymba_rglru_stickbreak_moe_step"

F32 = jnp.float32


def rmsnorm(x, g):
    xf = x.astype(F32)
    y = xf * lax.rsqrt(jnp.mean(xf * xf, axis=-1, keepdims=True) + EPS)
    return (y * g.astype(F32)).astype(x.dtype)


def causal_conv(x, state, w, b):
    L = x.shape[1]
    xp = jnp.concatenate([state.astype(x.dtype), x], axis=1)
    out = b + sum(w[j] * xp[:, j:j + L] for j in range(CONV_WIDTH))
    return out, xp[:, xp.shape[1] - (CONV_WIDTH - 1):]


def rg_lru(x, h0, w_a, b_a, w_i, b_i, lam):
    B, L, C = x.shape
    xb = x.reshape(B, L, LRU_BLOCKS, LRU_BLOCK)
    r = jax.nn.sigmoid(jnp.einsum('blnc,ncd->blnd', xb, w_a).reshape(B, L, C).astype(F32) + b_a.astype(F32))
    ig = jax.nn.sigmoid(jnp.einsum('blnc,ncd->blnd', xb, w_i).reshape(B, L, C).astype(F32) + b_i.astype(F32))
    log_a = -LRU_C * r * jax.nn.softplus(-lam.astype(F32))
    a = jnp.exp(log_a)
    u = jnp.sqrt(-jnp.expm1(2.0 * log_a)) * ig * x.astype(F32)

    def step(h, inp):
        a_t, u_t = inp
        h = a_t * h + u_t
        return h, h

    hT, hs = lax.scan(step, h0.astype(F32), (a.swapaxes(0, 1), u.swapaxes(0, 1)))
    return hs.swapaxes(0, 1).astype(x.dtype), hT.astype(x.dtype)


def stick_breaking(q, k, v, bias, pos0):
    B, Lq, H, Dh = q.shape
    Lk = k.shape[1]
    qb = min(QUERY_BLOCK, Lq)
    n_blk = -(-Lq // qb)
    pad = n_blk * qb - Lq
    qf = jnp.pad(q.astype(F32), ((0, 0), (0, pad), (0, 0), (0, 0)))
    qf = qf.reshape(B, n_blk, qb, H, Dh).transpose(1, 0, 2, 3, 4)
    kf = k.astype(F32)
    vf = v.astype(F32)
    bf = bias.astype(F32)[None, :, None, None]
    kpos = jnp.arange(Lk)
    scale = Dh ** -0.5

    def block(args):
        qblk, b_idx = args
        z = jnp.einsum('bqhd,bkhd->bhqk', qblk, kf) * scale + bf
        qpos = pos0 + b_idx * qb + jnp.arange(qb)
        mask = kpos[None, :] < qpos[:, None]
        log_keep = jnp.where(mask, jax.nn.log_sigmoid(-z), 0.0)
        log_after = lax.cumsum(log_keep, axis=3, reverse=True) - log_keep
        wts = jnp.where(mask, jnp.exp(jax.nn.log_sigmoid(z) + log_after), 0.0)
        return jnp.einsum('bhqk,bkhd->bqhd', wts, vf)

    out = lax.map(block, (qf, jnp.arange(n_blk)))
    out = out.transpose(1, 0, 2, 3, 4).reshape(B, n_blk * qb, H, Dh)[:, :Lq]
    return out.astype(v.dtype)


def moe(u, w_router, e_bias, w_eg, w_eu, w_ed, w_sg, w_su, w_sd):
    shp = u.shape
    t = u.reshape(-1, shp[-1])
    T = t.shape[0]
    scores = jax.nn.sigmoid(t.astype(F32) @ w_router.astype(F32))
    biased = scores + e_bias.astype(F32)
    grp = biased.reshape(T, N_GROUPS, N_EXPERTS // N_GROUPS)
    grp_score = lax.top_k(grp, 2)[0].sum(-1)
    _, gidx = lax.top_k(grp_score, TOPK_GROUPS)
    gmask = jnp.sum(jax.nn.one_hot(gidx, N_GROUPS, dtype=F32), axis=-2) > 0
    emask = jnp.repeat(gmask, N_EXPERTS // N_GROUPS, axis=-1)
    _, eidx = lax.top_k(jnp.where(emask, biased, -jnp.inf), TOP_K)
    sel = jnp.take_along_axis(scores, eidx, axis=-1)
    wsel = sel / jnp.sum(sel, axis=-1, keepdims=True) * ROUTED_SCALE
    gates = jnp.sum(jax.nn.one_hot(eidx, N_EXPERTS, dtype=F32) * wsel[..., None], axis=-2)

    chunk = min(MOE_CHUNK, T)
    n = -(-T // chunk)
    pad = n * chunk - T
    tp = jnp.pad(t, ((0, pad), (0, 0))).reshape(n, chunk, shp[-1])
    gp = jnp.pad(gates, ((0, pad), (0, 0))).reshape(n, chunk, N_EXPERTS)

    def expert_block(args):
        xb, gb = args
        hg = jnp.einsum('cd,edf->cef', xb, w_eg)
        hu = jnp.einsum('cd,edf->cef', xb, w_eu)
        hact = jax.nn.silu(hg) * hu * gb[..., None].astype(xb.dtype)
        return jnp.einsum('cef,efd->cd', hact, w_ed)

    routed = lax.map(expert_block, (tp, gp)).reshape(n * chunk, shp[-1])[:T]
    shared = (jax.nn.silu(t @ w_sg) * (t @ w_su)) @ w_sd
    return (routed + shared).reshape(shp)


def hybrid_layer(x, p, k_past, v_past, conv_state, h_state,
                 g_mix, w_in, conv_w, conv_b, w_a, b_a, w_i, b_i, lru_lambda,
                 q_gain, k_gain, sb_bias, g_lru_out, g_att_out, w_o, g_ffn,
                 w_router, e_bias, w_eg, w_eu, w_ed, w_sg, w_su, w_sd,
                 g_ple, w_ple_gate, w_ple_proj):
    B, L, _ = x.shape
    pos0 = k_past.shape[1]
    u = rmsnorm(x, g_mix)
    z = u @ w_in
    c1 = LRU_WIDTH
    c2 = 2 * LRU_WIDTH
    xr, yr, q, k, v = jnp.split(z, [c1, c2, c2 + ATT_WIDTH, c2 + 2 * ATT_WIDTH], axis=-1)
    xc, conv_new = causal_conv(xr, conv_state, conv_w, conv_b)
    hs, h_new = rg_lru(xc, h_state, w_a, b_a, w_i, b_i, lru_lambda)
    lru_out = hs * jax.nn.gelu(yr)
    q = rmsnorm(q.reshape(B, L, ATT_HEADS, HEAD_DIM), q_gain)
    k = rmsnorm(k.reshape(B, L, ATT_HEADS, HEAD_DIM), k_gain)
    v = v.reshape(B, L, ATT_HEADS, HEAD_DIM)
    keys = jnp.concatenate([k_past.astype(k.dtype), k], axis=1)
    vals = jnp.concatenate([v_past.astype(v.dtype), v], axis=1)
    att = stick_breaking(q, keys, vals, sb_bias, pos0).reshape(B, L, ATT_WIDTH)
    mix = jnp.concatenate([rmsnorm(lru_out, g_lru_out), rmsnorm(att, g_att_out)], axis=-1) @ w_o
    h = x + mix
    h = h + moe(rmsnorm(h, g_ffn), w_router, e_bias, w_eg, w_eu, w_ed, w_sg, w_su, w_sd)
    gate = jax.nn.sigmoid(rmsnorm(h, g_ple) @ w_ple_gate)
    h = h + gate * (p.astype(h.dtype) @ w_ple_proj)
    return h, k, v, conv_new, h_new


def _normal(key, shape, scale):
    return jax.random.normal(key, shape, F32) * scale


def _gain(key, shape):
    return 1.0 + 0.02 * jax.random.normal(key, shape, F32)


def setup_inputs(seed: int = 0) -> dict:
    key = jax.random.key(seed)
    ks = list(jax.random.split(key, 40))
    n_pages = PAST_LEN // PAGE_SIZE
    n_used = DEC_BATCH * n_pages
    n_pool = n_used + max(1, n_used // 4)
    d = D_MODEL
    a0 = jax.random.uniform(ks[30], (DEPTH, LRU_WIDTH), F32, minval=0.9, maxval=0.999)
    a_base = a0 ** (1.0 / LRU_C)
    lru_lambda = jnp.log(a_base) - jnp.log1p(-a_base)
    page_table = jax.random.permutation(ks[31], n_pool)[:n_used].reshape(DEC_BATCH, n_pages).astype(jnp.int32)
    return {
        "x_prompt": _normal(ks[0], (BATCH, SEQ, d), 1.0),
        "x_sample": _normal(ks[1], (DEC_BATCH, DEC_SEQ, d), 1.0),
        "p_prompt": _normal(ks[2], (DEPTH, BATCH, SEQ, PLE_DIM), 1.0),
        "p_sample": _normal(ks[3], (DEPTH, DEC_BATCH, DEC_SEQ, PLE_DIM), 1.0),
        "cache_k": _normal(ks[4], (DEPTH, n_pool, PAGE_SIZE, ATT_HEADS, HEAD_DIM), 1.0),
        "cache_v": _normal(ks[5], (DEPTH, n_pool, PAGE_SIZE, ATT_HEADS, HEAD_DIM), 1.0),
        "state_conv": _normal(ks[6], (DEPTH, DEC_BATCH, CONV_WIDTH - 1, LRU_WIDTH), 1.0),
        "state_h": _normal(ks[7], (DEPTH, DEC_BATCH, LRU_WIDTH), 0.5),
        "page_table": page_table,
        "g_mix": _gain(ks[8], (DEPTH, d)),
        "w_in": _normal(ks[9], (DEPTH, d, IN_WIDTH), d ** -0.5),
        "conv_w": _normal(ks[10], (DEPTH, CONV_WIDTH, LRU_WIDTH), 0.5),
        "conv_b": _normal(ks[11], (DEPTH, LRU_WIDTH), 0.02),
        "w_a": _normal(ks[12], (DEPTH, LRU_BLOCKS, LRU_BLOCK, LRU_BLOCK), LRU_BLOCK ** -0.5),
        "b_a": _normal(ks[13], (DEPTH, LRU_WIDTH), 0.02),
        "w_i": _normal(ks[14], (DEPTH, LRU_BLOCKS, LRU_BLOCK, LRU_BLOCK), LRU_BLOCK ** -0.5),
        "b_i": _normal(ks[15], (DEPTH, LRU_WIDTH), 0.02),
        "lru_lambda": lru_lambda,
        "q_gain": _gain(ks[16], (DEPTH, HEAD_DIM)),
        "k_gain": _gain(ks[17], (DEPTH, HEAD_DIM)),
        "sb_bias": SB_BIAS_INIT + 0.1 * jax.random.normal(ks[35], (DEPTH, ATT_HEADS), F32),
        "g_lru_out": _gain(ks[18], (DEPTH, LRU_WIDTH)),
        "g_att_out": _gain(ks[19], (DEPTH, ATT_WIDTH)),
        "w_o": _normal(ks[20], (DEPTH, MIX_WIDTH, d), MIX_WIDTH ** -0.5),
        "g_ffn": _gain(ks[21], (DEPTH, d)),
        "w_router": _normal(ks[22], (DEPTH, d, N_EXPERTS), d ** -0.5),
        "e_bias": _normal(ks[23], (DEPTH, N_EXPERTS), 0.01),
        "w_eg": _normal(ks[24], (DEPTH, N_EXPERTS, d, EXPERT_DIM), d ** -0.5),
        "w_eu": _normal(ks[25], (DEPTH, N_EXPERTS, d, EXPERT_DIM), d ** -0.5),
        "w_ed": _normal(ks[26], (DEPTH, N_EXPERTS, EXPERT_DIM, d), EXPERT_DIM ** -0.5),
        "w_sg": _normal(ks[27], (DEPTH, d, SHARED_DIM), d ** -0.5),
        "w_su": _normal(ks[28], (DEPTH, d, SHARED_DIM), d ** -0.5),
        "w_sd": _normal(ks[29], (DEPTH, SHARED_DIM, d), SHARED_DIM ** -0.5),
        "g_ple": _gain(ks[32], (DEPTH, d)),
        "w_ple_gate": _normal(ks[33], (DEPTH, d, d), d ** -0.5),
        "w_ple_proj": _normal(ks[34], (DEPTH, PLE_DIM, d), PLE_DIM ** -0.5),
    }


def reference(x_prompt, x_sample, p_prompt, p_sample, cache_k, cache_v, state_conv, state_h, page_table,
              g_mix, w_in, conv_w, conv_b, w_a, b_a, w_i, b_i, lru_lambda,
              q_gain, k_gain, sb_bias, g_lru_out, g_att_out, w_o, g_ffn,
              w_router, e_bias, w_eg, w_eu, w_ed, w_sg, w_su, w_sd,
              g_ple, w_ple_gate, w_ple_proj):
    n_pages = PAST_LEN // PAGE_SIZE
    hp = x_prompt
    hsmp = x_sample
    kp_l, vp_l, cp_l, hp_l = [], [], [], []
    ks_l, vs_l, cs_l, hs_l = [], [], [], []
    for i in range(DEPTH):
        w = (g_mix[i], w_in[i], conv_w[i], conv_b[i], w_a[i], b_a[i], w_i[i], b_i[i], lru_lambda[i],
             q_gain[i], k_gain[i], sb_bias[i], g_lru_out[i], g_att_out[i], w_o[i], g_ffn[i],
             w_router[i], e_bias[i], w_eg[i], w_eu[i], w_ed[i], w_sg[i], w_su[i], w_sd[i],
             g_ple[i], w_ple_gate[i], w_ple_proj[i])
        k0 = jnp.zeros((BATCH, 0, ATT_HEADS, HEAD_DIM), hp.dtype)
        c0 = jnp.zeros((BATCH, CONV_WIDTH - 1, LRU_WIDTH), hp.dtype)
        h0 = jnp.zeros((BATCH, LRU_WIDTH), hp.dtype)
        hp, kp, vp, cp, hfin = hybrid_layer(hp, p_prompt[i], k0, k0, c0, h0, *w)
        k_past = cache_k[i][page_table].reshape(DEC_BATCH, n_pages * PAGE_SIZE, ATT_HEADS, HEAD_DIM)
        v_past = cache_v[i][page_table].reshape(DEC_BATCH, n_pages * PAGE_SIZE, ATT_HEADS, HEAD_DIM)
        hsmp, ksn, vsn, csn, hsn = hybrid_layer(hsmp, p_sample[i], k_past, v_past, state_conv[i], state_h[i], *w)
        kp_l.append(kp); vp_l.append(vp); cp_l.append(cp); hp_l.append(hfin)
        ks_l.append(ksn); vs_l.append(vsn); cs_l.append(csn); hs_l.append(hsn)
    k_prompt = jnp.stack(kp_l)
    v_prompt = jnp.stack(vp_l)
    conv_prompt = jnp.stack(cp_l)
    h_prompt = jnp.stack(hp_l)
    k_sample = jnp.stack(ks_l)
    v_sample = jnp.stack(vs_l)
    conv_sample = jnp.stack(cs_l)
    h_sample = jnp.stack(hs_l)
    return (hp, hsmp, k_prompt, v_prompt, conv_prompt, h_prompt, k_sample, v_sample, conv_sample, h_sample)
```

```python
import functools

import jax
import jax.numpy as jnp
from jax import lax
from jax.experimental import pallas as pl
from jax.experimental.pallas import tpu as pltpu

F32 = jnp.float32
BF16 = jnp.bfloat16
SDS = jax.ShapeDtypeStruct

EPS = 1e-6
LRU_C = 8.0
N_GROUPS = 8
TOPK_GROUPS = 4
TOP_K = 8
ROUTED_SCALE = 2.5

VMEM_LIMIT_BYTES = 56 * 1024 * 1024

NT_DIMS = (((1,), (1,)), ((), ()))


def _params(n_grid_axes):
    return pltpu.CompilerParams(
        dimension_semantics=("arbitrary",) * n_grid_axes,
        vmem_limit_bytes=VMEM_LIMIT_BYTES,
    )


def _rms(x, g):
    return x * lax.rsqrt(jnp.mean(x * x, axis=-1, keepdims=True) + EPS) * g


def _softplus(x):
    return jnp.maximum(x, 0.0) + jnp.log1p(jnp.exp(-jnp.abs(x)))


def _silu(x):
    return x * jax.nn.sigmoid(x)


def _dot(a, b):
    return jnp.dot(a, b, preferred_element_type=F32)


def _head_rms(z, gain, n_heads):
    dh = z.shape[1] // n_heads
    return jnp.concatenate(
        [_rms(z[:, h * dh:(h + 1) * dh], gain) for h in range(n_heads)], axis=-1)


def _in_proj_kernel(x_ref, g_ref, w_ref, qg_ref, kg_ref,
                    xr_ref, gy_ref, q_ref, k_ref, v_ref, kb_ref, vb_ref,
                    xn_ref, *, n_heads):
    j = pl.program_id(1)

    @pl.when(j == 0)
    def _():
        xn_ref[...] = _rms(x_ref[...], g_ref[...]).astype(BF16)

    def z():
        return _dot(xn_ref[...], w_ref[...])

    @pl.when(j == 0)
    def _():
        xr_ref[...] = z()

    @pl.when(j == 1)
    def _():
        gy_ref[...] = jax.nn.gelu(z())

    @pl.when(j == 2)
    def _():
        dh = q_ref.shape[1] // n_heads
        q_ref[...] = (_head_rms(z(), qg_ref[...], n_heads) * dh ** -0.5).astype(BF16)

    @pl.when(j == 3)
    def _():
        k = _head_rms(z(), kg_ref[...], n_heads)
        k_ref[...] = k
        kb_ref[...] = k.astype(BF16)

    @pl.when(j == 4)
    def _():
        v = z()
        v_ref[...] = v
        vb_ref[...] = v.astype(BF16)


def _in_proj(x, g_mix, w_in_bf, q_gain, k_gain, n_heads, tm):
    t, d = x.shape
    c = w_in_bf.shape[1] // 5
    dh = c // n_heads
    tm = min(tm, t)
    assert t % tm == 0
    row = lambda i, j: (i, 0)
    const = lambda i, j: (0, 0)
    out_blk = pl.BlockSpec((tm, c), row)
    return pl.pallas_call(
        functools.partial(_in_proj_kernel, n_heads=n_heads),
        grid=(t // tm, 5),
        in_specs=[
            pl.BlockSpec((tm, d), row),
            pl.BlockSpec((1, d), const),
            pl.BlockSpec((d, c), lambda i, j: (0, j)),
            pl.BlockSpec((1, dh), const),
            pl.BlockSpec((1, dh), const),
        ],
        out_specs=[out_blk] * 7,
        out_shape=[SDS((t, c), F32), SDS((t, c), F32), SDS((t, c), BF16),
                   SDS((t, c), F32), SDS((t, c), F32), SDS((t, c), BF16),
                   SDS((t, c), BF16)],
        scratch_shapes=[pltpu.VMEM((tm, d), BF16)],
        compiler_params=_params(2),
        name="in_proj",
    )(x, g_mix, w_in_bf, q_gain, k_gain)


def _lru_gates(xc, wai_ref, ba, bi, lam):
    n_blocks, blk, _ = wai_ref.shape
    xcb = xc.astype(BF16)
    ga, gi = [], []
    for n in range(n_blocks):
        g = _dot(xcb[:, n * blk:(n + 1) * blk], wai_ref[n])
        ga.append(g[:, :blk])
        gi.append(g[:, blk:])
    r = jax.nn.sigmoid(jnp.concatenate(ga, axis=-1) + ba)
    ig = jax.nn.sigmoid(jnp.concatenate(gi, axis=-1) + bi)
    log_a = -LRU_C * r * _softplus(-lam)
    a = jnp.exp(log_a)
    th = jnp.tanh(log_a)
    u = jnp.sqrt(-2.0 * th / (1.0 - th)) * ig * xc
    return a, u


def _lru_prompt_kernel(xr_ref, gy_ref, st_ref, h0_ref, cw_ref, cb_ref, wai_ref,
                       ba_ref, bi_ref, lam_ref, gl_ref, mix_ref, ht_ref,
                       xext_ref, a_ref, u_ref, h_ref, *, tl):
    l = pl.program_id(1)

    @pl.when(l == 0)
    def _():
        xext_ref[0:8, :] = st_ref[0]
        h_ref[...] = h0_ref[0]

    xext_ref[8:8 + tl, :] = xr_ref[0]
    cw = cw_ref[...]
    n_taps = cw.shape[0]
    xc = cb_ref[...]
    for j in range(n_taps):
        off = 8 - (n_taps - 1) + j
        xc = xc + cw[j:j + 1, :] * xext_ref[off:off + tl, :]
    xext_ref[0:8, :] = xext_ref[tl:tl + 8, :]

    a, u = _lru_gates(xc, wai_ref, ba_ref[...], bi_ref[...], lam_ref[...])
    a_ref[...] = a
    u_ref[...] = u

    def step(t, h):
        h = a_ref[pl.ds(t, 1), :] * h + u_ref[pl.ds(t, 1), :]
        u_ref[pl.ds(t, 1), :] = h
        return h

    h = lax.fori_loop(0, tl, step, h_ref[...], unroll=8)
    h_ref[...] = h
    mix_ref[0] = _rms(u_ref[...] * gy_ref[0], gl_ref[...]).astype(BF16)

    @pl.when(l == pl.num_programs(1) - 1)
    def _():
        ht_ref[0] = h


def _lru_prompt(xr, gy, state8, h0, conv_w, conv_b, wai, b_a, b_i, lam, g_lru, tl):
    b, l, c = xr.shape
    tl = min(tl, l)
    assert l % tl == 0 and tl % 8 == 0
    tile = pl.BlockSpec((1, tl, c), lambda i, j: (i, j, 0))
    per_b = lambda rows: pl.BlockSpec((1, rows, c), lambda i, j: (i, 0, 0))
    vec = pl.BlockSpec((1, c), lambda i, j: (0, 0))
    return pl.pallas_call(
        functools.partial(_lru_prompt_kernel, tl=tl),
        grid=(b, l // tl),
        in_specs=[tile, tile, per_b(8), per_b(1),
                  pl.BlockSpec(conv_w.shape, lambda i, j: (0, 0)), vec,
                  pl.BlockSpec(wai.shape, lambda i, j: (0, 0, 0)),
                  vec, vec, vec, vec],
        out_specs=[tile, per_b(1)],
        out_shape=[SDS((b, l, c), BF16), SDS((b, 1, c), F32)],
        scratch_shapes=[pltpu.VMEM((tl + 8, c), F32), pltpu.VMEM((tl, c), F32),
                        pltpu.VMEM((tl, c), F32), pltpu.VMEM((1, c), F32)],
        compiler_params=_params(2),
        name="lru_prompt",
    )(xr, gy, state8, h0, conv_w, conv_b, wai, b_a, b_i, lam, g_lru)


def _lru_step_kernel(xr_ref, gy_ref, st_ref, h0_ref, cw_ref, cb_ref, wai_ref,
                     ba_ref, bi_ref, lam_ref, gl_ref, mix_ref, hn_ref):
    cw = cw_ref[...]
    n_taps = cw.shape[0]
    xc = cb_ref[...] + cw[n_taps - 1:n_taps, :] * xr_ref[...]
    for j in range(n_taps - 1):
        xc = xc + cw[j:j + 1, :] * st_ref[j]
    a, u = _lru_gates(xc, wai_ref, ba_ref[...], bi_ref[...], lam_ref[...])
    h = a * h0_ref[...] + u
    hn_ref[...] = h
    mix_ref[...] = _rms(h * gy_ref[...], gl_ref[...]).astype(BF16)


def _lru_step(xr, gy, state, h0, conv_w, conv_b, wai, b_a, b_i, lam, g_lru):
    b, c = xr.shape
    return pl.pallas_call(
        _lru_step_kernel,
        out_shape=[SDS((b, c), BF16), SDS((b, c), F32)],
        compiler_params=pltpu.CompilerParams(vmem_limit_bytes=VMEM_LIMIT_BYTES),
        name="lru_step",
    )(xr, gy, state, h0, conv_w, conv_b, wai, b_a, b_i, lam, g_lru)


def _attn_prompt_kernel(bias_ref, q_ref, k_ref, v_ref, tri_ref, o_ref, *, tq):
    h = pl.program_id(1)
    i = pl.program_id(2)
    bias = bias_ref[h]
    q = q_ref[...]
    tri = tri_ref[...]
    dh = q.shape[1]

    def tile(j, carry, diagonal):
        c, acc = carry
        start = pl.multiple_of(j * tq, tq)
        kj = k_ref[pl.ds(start, tq), :]
        vj = v_ref[pl.ds(start, tq), :]
        z = lax.dot_general(q, kj, NT_DIMS, preferred_element_type=F32) + bias
        sp = _softplus(z)
        log_keep = -sp
        if diagonal:
            row = lax.broadcasted_iota(jnp.int32, (tq, tq), 0)
            col = lax.broadcasted_iota(jnp.int32, (tq, tq), 1)
            causal = col < row
            log_keep = jnp.where(causal, log_keep, 0.0)
        log_after = _dot(log_keep.astype(BF16), tri)
        w = jnp.exp(z - sp + log_after + c)
        if diagonal:
            w = jnp.where(causal, w, 0.0)
        acc = acc + _dot(w.astype(BF16), vj)
        c = c + log_after[:, 0:1] + log_keep[:, 0:1]
        return c, acc

    carry = (jnp.zeros((tq, 1), F32), jnp.zeros((tq, dh), F32))
    carry = tile(i, carry, True)
    carry = lax.fori_loop(0, i, lambda jj, cr: tile(i - 1 - jj, cr, False), carry)
    o_ref[...] = carry[1]


def _attn_prompt(q, k, v, sb_bias, batch, n_heads, tq):
    t, c = q.shape
    l = t // batch
    dh = c // n_heads
    tq = min(tq, l)
    assert l % tq == 0
    nq = l // tq
    tri = (jnp.arange(tq)[:, None] > jnp.arange(tq)[None, :]).astype(BF16)
    kv_spec = pl.BlockSpec((l, dh), lambda b, h, i: (b, h))
    q_spec = pl.BlockSpec((tq, dh), lambda b, h, i: (b * nq + i, h))
    return pl.pallas_call(
        functools.partial(_attn_prompt_kernel, tq=tq),
        grid=(batch, n_heads, nq),
        in_specs=[pl.BlockSpec(memory_space=pltpu.SMEM), q_spec, kv_spec, kv_spec,
                  pl.BlockSpec((tq, tq), lambda b, h, i: (0, 0))],
        out_specs=q_spec,
        out_shape=SDS((t, c), F32),
        compiler_params=_params(3),
        name="attn_prompt",
    )(sb_bias, q, k, v, tri)


def _shift_left(x, s):
    n = x.shape[1]
    col = lax.broadcasted_iota(jnp.int32, x.shape, 1)
    return jnp.where(col < n - s, pltpu.roll(x, n - s, axis=1), 0.0)


def _attn_sample_kernel(pt_ref, bias_ref, q_ref, *refs, n_pages_step):
    del pt_ref
    k_refs = refs[:n_pages_step]
    v_refs = refs[n_pages_step:2 * n_pages_step]
    o_ref, c_ref, acc_ref = refs[2 * n_pages_step:]
    step = pl.program_id(1)

    @pl.when(step == 0)
    def _():
        c_ref[...] = jnp.zeros_like(c_ref)
        acc_ref[...] = jnp.zeros_like(acc_ref)

    q = q_ref[0]
    n_heads, dh = q.shape
    page = k_refs[0].shape[0]
    n_col = page * n_heads
    row = lax.broadcasted_iota(jnp.int32, (n_heads, n_col), 0)
    col = lax.broadcasted_iota(jnp.int32, (n_heads, n_col), 1)
    own = lax.rem(col, n_heads) == row
    bias = bias_ref[...]
    c = c_ref[...]
    acc = acc_ref[...]
    for g in range(n_pages_step):
        kp = k_refs[g][...].reshape(n_col, dh).astype(BF16)
        vp = v_refs[g][...].reshape(n_col, dh).astype(BF16)
        z = lax.dot_general(q, kp, NT_DIMS, preferred_element_type=F32) + bias
        sp = _softplus(z)
        log_keep = jnp.where(own, -sp, 0.0)
        log_after = _shift_left(log_keep, n_heads)
        s = n_heads
        while s < n_col:
            log_after = log_after + _shift_left(log_after, s)
            s *= 2
        w = jnp.where(own, jnp.exp(z - sp + log_after + c), 0.0)
        acc = acc + _dot(w.astype(BF16), vp)
        c = c + jnp.sum(log_keep, axis=1, keepdims=True)
    c_ref[...] = c
    acc_ref[...] = acc

    @pl.when(step == pl.num_programs(1) - 1)
    def _():
        o_ref[0] = acc


def _attn_sample(q, cache_k, cache_v, page_table, sb_bias, n_pages_step):
    b, n_heads, dh = q.shape
    _, page, _, _ = cache_k.shape
    n_pages = page_table.shape[1]
    g_ = min(n_pages_step, n_pages)
    assert n_pages % g_ == 0

    def page_spec(g):
        return pl.BlockSpec(
            (None, page, n_heads, dh),
            lambda i, j, pt: (pt[i, n_pages - 1 - (j * g_ + g)], 0, 0, 0))

    q_spec = pl.BlockSpec((1, n_heads, dh), lambda i, j, pt: (i, 0, 0))
    grid_spec = pltpu.PrefetchScalarGridSpec(
        num_scalar_prefetch=1,
        grid=(b, n_pages // g_),
        in_specs=[pl.BlockSpec((n_heads, 1), lambda i, j, pt: (0, 0)), q_spec]
        + [page_spec(g) for g in range(g_)] * 2,
        out_specs=q_spec,
        scratch_shapes=[pltpu.VMEM((n_heads, 1), F32), pltpu.VMEM((n_heads, dh), F32)],
    )
    return pl.pallas_call(
        functools.partial(_attn_sample_kernel, n_pages_step=g_),
        grid_spec=grid_spec,
        out_shape=SDS((b, n_heads, dh), F32),
        compiler_params=_params(2),
        name="attn_sample",
    )(page_table, sb_bias.reshape(n_heads, 1), q,
      *([cache_k] * g_), *([cache_v] * g_))


def _out_proj_kernel(ml_ref, att_ref, x_ref, ga_ref, wo_ref, gf_ref, wr_ref,
                     h_ref, hn_ref, lg_ref):
    c = ml_ref.shape[1]
    attn = _rms(att_ref[...], ga_ref[...]).astype(BF16)
    mix = _dot(ml_ref[...], wo_ref[0:c, :]) + _dot(attn, wo_ref[c:2 * c, :])
    h = x_ref[...] + mix
    h_ref[...] = h
    hn = _rms(h, gf_ref[...])
    hn_ref[...] = hn.astype(BF16)
    lg_ref[...] = lax.dot_general(wr_ref[...], hn, NT_DIMS,
                                  precision=lax.Precision.HIGHEST,
                                  preferred_element_type=F32)


def _out_proj(mix_lru, att, x, g_att, w_o_bf, g_ffn, w_router_t, tm):
    t, d = x.shape
    c = mix_lru.shape[1]
    e = w_router_t.shape[0]
    tm = min(tm, t)
    assert t % tm == 0
    row = lambda w: pl.BlockSpec((tm, w), lambda i: (i, 0))
    full = lambda a: pl.BlockSpec(a.shape, lambda i: (0, 0))
    return pl.pallas_call(
        _out_proj_kernel,
        grid=(t // tm,),
        in_specs=[row(c), row(c), row(d), full(g_att), full(w_o_bf), full(g_ffn),
                  full(w_router_t)],
        out_specs=[row(d), row(d), pl.BlockSpec((e, tm), lambda i: (0, i))],
        out_shape=[SDS((t, d), F32), SDS((t, d), BF16), SDS((e, t), F32)],
        compiler_params=_params(1),
        name="out_proj",
    )(mix_lru, att, x, g_att, w_o_bf, g_ffn, w_router_t)


def _take_top(cur, idx, sentinel, n):
    picked = jnp.zeros(cur.shape, jnp.bool_)
    for _ in range(n):
        m = jnp.max(cur, axis=0, keepdims=True)
        first = jnp.min(jnp.where(cur == m, idx, sentinel), axis=0, keepdims=True)
        pick = idx == first
        picked = jnp.logical_or(picked, pick)
        cur = jnp.where(pick, -jnp.inf, cur)
    return picked


def _router_kernel(lg_ref, eb_ref, gate_ref):
    e, tm = lg_ref.shape
    per = e // N_GROUPS
    scores = jax.nn.sigmoid(lg_ref[...])
    biased = scores + eb_ref[...]
    sub = lax.broadcasted_iota(jnp.int32, (per, tm), 0)
    group_scores = []
    for g in range(N_GROUPS):
        bg = biased[g * per:(g + 1) * per, :]
        m1 = jnp.max(bg, axis=0, keepdims=True)
        first = jnp.min(jnp.where(bg == m1, sub, per), axis=0, keepdims=True)
        m2 = jnp.max(jnp.where(sub == first, -jnp.inf, bg), axis=0, keepdims=True)
        group_scores.append(m1 + m2)
    gs = jnp.concatenate(group_scores, axis=0)
    gidx = lax.broadcasted_iota(jnp.int32, gs.shape, 0)
    gsel = _take_top(gs, gidx, N_GROUPS, TOPK_GROUPS)
    emask = jnp.concatenate(
        [jnp.broadcast_to(gsel[g:g + 1, :], (per, tm)) for g in range(N_GROUPS)],
        axis=0)
    eidx = lax.broadcasted_iota(jnp.int32, (e, tm), 0)
    sel = _take_top(jnp.where(emask, biased, -jnp.inf), eidx, e, TOP_K)
    chosen = jnp.where(sel, scores, 0.0)
    gate_ref[...] = chosen / jnp.sum(chosen, axis=0, keepdims=True) * ROUTED_SCALE


def _router(logits_t, e_bias, tm):
    e, t = logits_t.shape
    tm = min(tm, t)
    assert t % tm == 0
    blk = pl.BlockSpec((e, tm), lambda i: (0, i))
    return pl.pallas_call(
        _router_kernel,
        grid=(t // tm,),
        in_specs=[blk, pl.BlockSpec((e, 1), lambda i: (0, 0))],
        out_specs=blk,
        out_shape=SDS((e, t), F32),
        compiler_params=_params(1),
        name="router",
    )(logits_t, e_bias.reshape(e, 1))


def _moe_kernel(x_ref, gate_ref, wg_ref, wu_ref, wd_ref, o_ref):
    e = pl.program_id(1)

    @pl.when(e == 0)
    def _():
        o_ref[...] = jnp.zeros_like(o_ref)

    x = x_ref[...]
    gates = gate_ref[...]
    lane = lax.broadcasted_iota(jnp.int32, gates.shape, 1)
    g = jnp.sum(jnp.where(lane == e, gates, 0.0), axis=1, keepdims=True)
    act = _silu(_dot(x, wg_ref[...])) * _dot(x, wu_ref[...]) * g
    o_ref[...] += _dot(act.astype(BF16), wd_ref[...])


def _moe(hn, gates, w_eg_bf, w_eu_bf, w_ed_bf, tm):
    t, d = hn.shape
    e, _, f = w_eg_bf.shape
    tm = min(tm, t)
    assert t % tm == 0
    return pl.pallas_call(
        _moe_kernel,
        grid=(t // tm, e),
        in_specs=[pl.BlockSpec((tm, d), lambda i, j: (i, 0)),
                  pl.BlockSpec((tm, e), lambda i, j: (i, 0)),
                  pl.BlockSpec((None, d, f), lambda i, j: (j, 0, 0)),
                  pl.BlockSpec((None, d, f), lambda i, j: (j, 0, 0)),
                  pl.BlockSpec((None, f, d), lambda i, j: (j, 0, 0))],
        out_specs=pl.BlockSpec((tm, d), lambda i, j: (i, 0)),
        out_shape=SDS((t, d), F32),
        compiler_params=_params(2),
        name="moe",
    )(hn, gates, w_eg_bf, w_eu_bf, w_ed_bf)


def _final_kernel(h_ref, r_ref, hn_ref, p_ref, wsg_ref, wsu_ref, wsd_ref, gp_ref,
                  wpg_ref, wpp_ref, y_ref):
    hn = hn_ref[...]
    act = _silu(_dot(hn, wsg_ref[...])) * _dot(hn, wsu_ref[...])
    h2 = h_ref[...] + r_ref[...] + _dot(act.astype(BF16), wsd_ref[...])
    gate = jax.nn.sigmoid(_dot(_rms(h2, gp_ref[...]).astype(BF16), wpg_ref[...]))
    y_ref[...] = h2 + gate * _dot(p_ref[...].astype(BF16), wpp_ref[...])


def _final(h, routed, hn, p, w_sg, w_su, w_sd, g_ple, w_pg, w_pp, tm):
    t, d = h.shape
    tm = min(tm, t)
    assert t % tm == 0
    row = lambda w: pl.BlockSpec((tm, w), lambda i: (i, 0))
    full = lambda a: pl.BlockSpec(a.shape, lambda i: (0, 0))
    return pl.pallas_call(
        _final_kernel,
        grid=(t // tm,),
        in_specs=[row(d), row(d), row(d), row(p.shape[1]), full(w_sg), full(w_su),
                  full(w_sd), full(g_ple), full(w_pg), full(w_pp)],
        out_specs=row(d),
        out_shape=SDS((t, d), F32),
        compiler_params=_params(1),
        name="final",
    )(h, routed, hn, p, w_sg, w_su, w_sd, g_ple, w_pg, w_pp)


def _channel_mixer(mix_lru, att, x, p, wts):
    h, hn, logits_t = _out_proj(mix_lru, att, x, wts["g_att"], wts["w_o"],
                                wts["g_ffn"], wts["w_router_t"], tm=256)
    gates = _router(logits_t, wts["e_bias"], tm=256).T
    routed = _moe(hn, gates, wts["w_eg"], wts["w_eu"], wts["w_ed"], tm=512)
    return _final(h, routed, hn, p, wts["w_sg"], wts["w_su"], wts["w_sd"],
                  wts["g_ple"], wts["w_pg"], wts["w_pp"], tm=256)


def kernel(x_prompt, x_sample, p_prompt, p_sample, cache_k, cache_v, state_conv, state_h, page_table, g_mix, w_in, conv_w, conv_b, w_a, b_a, w_i, b_i, lru_lambda, q_gain, k_gain, sb_bias, g_lru_out, g_att_out, w_o, g_ffn, w_router, e_bias, w_eg, w_eu, w_ed, w_sg, w_su, w_sd, g_ple, w_ple_gate, w_ple_proj):
    depth = g_mix.shape[0]
    assert depth == 1
    batch, seq, d = x_prompt.shape
    dec_batch, dec_seq, _ = x_sample.shape
    assert dec_seq == 1
    n_heads, dh = cache_k.shape[3], cache_k.shape[4]
    c = n_heads * dh
    n_taps = conv_w.shape[1]
    assert w_in.shape[2] == 5 * c and seq >= n_taps - 1

    row = lambda a: a.reshape(1, -1)
    wai = jnp.concatenate([w_a[0], w_i[0]], axis=-1).astype(BF16)
    lru_w = (conv_w[0], row(conv_b[0]), wai, row(b_a[0]), row(b_i[0]),
             row(lru_lambda[0]), row(g_lru_out[0]))
    w_in_bf = w_in[0].astype(BF16)
    wts = dict(
        g_att=row(g_att_out[0]), w_o=w_o[0].astype(BF16), g_ffn=row(g_ffn[0]),
        w_router_t=w_router[0].T, e_bias=e_bias[0],
        w_eg=w_eg[0].astype(BF16), w_eu=w_eu[0].astype(BF16), w_ed=w_ed[0].astype(BF16),
        w_sg=w_sg[0].astype(BF16), w_su=w_su[0].astype(BF16), w_sd=w_sd[0].astype(BF16),
        g_ple=row(g_ple[0]), w_pg=w_ple_gate[0].astype(BF16),
        w_pp=w_ple_proj[0].astype(BF16))
    in_proj = functools.partial(_in_proj, g_mix=row(g_mix[0]), w_in_bf=w_in_bf,
                                q_gain=row(q_gain[0]), k_gain=row(k_gain[0]),
                                n_heads=n_heads)

    xp = x_prompt.reshape(batch * seq, d)
    xr, gy, q, k, v, kb, vb = in_proj(xp, tm=512)
    mix_lru, h_last = _lru_prompt(
        xr.reshape(batch, seq, c), gy.reshape(batch, seq, c),
        jnp.zeros((batch, 8, c), F32), jnp.zeros((batch, 1, c), F32), *lru_w, tl=256)
    att = _attn_prompt(q, kb, vb, sb_bias[0], batch, n_heads, tq=256)
    y_prompt = _channel_mixer(mix_lru.reshape(batch * seq, c), att, xp,
                              p_prompt[0].reshape(batch * seq, -1), wts)
    k_prompt = k.reshape(1, batch, seq, n_heads, dh)
    v_prompt = v.reshape(1, batch, seq, n_heads, dh)
    conv_prompt = xr.reshape(batch, seq, c)[:, seq - (n_taps - 1):][None]
    h_prompt = h_last.reshape(1, batch, c)

    xs = x_sample.reshape(dec_batch, d)
    xr_s, gy_s, q_s, k_s, v_s, _, _ = in_proj(xs, tm=dec_batch)
    mix_lru_s, h_s = _lru_step(xr_s, gy_s, jnp.swapaxes(state_conv[0], 0, 1),
                               state_h[0], *lru_w)
    att_s = _attn_sample(q_s.reshape(dec_batch, n_heads, dh), cache_k[0], cache_v[0],
                         page_table, sb_bias[0], n_pages_step=8)
    y_sample = _channel_mixer(mix_lru_s, att_s.reshape(dec_batch, c), xs,
                              p_sample[0].reshape(dec_batch, -1), wts)
    k_sample = k_s.reshape(1, dec_batch, 1, n_heads, dh)
    v_sample = v_s.reshape(1, dec_batch, 1, n_heads, dh)
    conv_sample = jnp.concatenate([state_conv[0][:, 1:], xr_s[:, None]], axis=1)[None]
    h_sample = h_s.reshape(1, dec_batch, c)

    return (y_prompt.reshape(batch, seq, d), y_sample.reshape(dec_batch, 1, d),
            k_prompt, v_prompt, conv_prompt, h_prompt,
            k_sample, v_sample, conv_sample, h_sample)
```

```python
import functools

import jax
import jax.numpy as jnp
from jax import lax
from jax.experimental import pallas as pl
from jax.experimental.pallas import tpu as pltpu

F32 = jnp.float32
BF16 = jnp.bfloat16
U32 = jnp.uint32
I32 = jnp.int32
SDS = jax.ShapeDtypeStruct

EPS = 1e-6
LRU_C = 8.0
N_GROUPS = 8
TOPK_GROUPS = 4
TOP_K = 8
ROUTED_SCALE = 2.5

VMEM_LIMIT_BYTES = 56 * 1024 * 1024

NT_DIMS = (((1,), (1,)), ((), ()))


def _params(n_grid_axes):
    return pltpu.CompilerParams(
        dimension_semantics=("arbitrary",) * n_grid_axes,
        vmem_limit_bytes=VMEM_LIMIT_BYTES,
    )


def _rms(x, g):
    return x * lax.rsqrt(jnp.mean(x * x, axis=-1, keepdims=True) + EPS) * g


def _softplus(x):
    return jnp.maximum(x, 0.0) + jnp.log(1.0 + jnp.exp(-jnp.abs(x)))


def _silu(x):
    return x * jax.nn.sigmoid(x)


def _dot(a, b):
    return jnp.dot(a, b, preferred_element_type=F32)


def _pack_halves(x):
    half = x.shape[1] // 2
    lo = lax.bitcast_convert_type(x[:, :half].astype(BF16).astype(F32), U32)
    hi = lax.bitcast_convert_type(x[:, half:].astype(BF16).astype(F32), U32)
    return (lo >> 16) | hi


def _unpack_halves(w):
    lo = lax.bitcast_convert_type(w << 16, F32)
    hi = lax.bitcast_convert_type(w & jnp.uint32(0xFFFF0000), F32)
    return lo, hi


def _head_rms(z, gain, n_heads):
    dh = z.shape[1] // n_heads
    return jnp.concatenate(
        [_rms(z[:, h * dh:(h + 1) * dh], gain) for h in range(n_heads)], axis=-1)


def _in_proj_kernel(x_ref, g_ref, w_ref, qg_ref, kg_ref,
                    xr_ref, gy_ref, q_ref, k_ref, v_ref, kb_ref, vb_ref,
                    xn_ref, *, n_heads):
    j = pl.program_id(1)

    @pl.when(j == 0)
    def _():
        xn_ref[...] = _rms(x_ref[...], g_ref[...]).astype(BF16)

    def z():
        return _dot(xn_ref[...], w_ref[...])

    @pl.when(j == 0)
    def _():
        xr_ref[...] = z()

    @pl.when(j == 1)
    def _():
        gy_ref[...] = jax.nn.gelu(z())

    @pl.when(j == 2)
    def _():
        dh = q_ref.shape[1] // n_heads
        q_ref[...] = (_head_rms(z(), qg_ref[...], n_heads) * dh ** -0.5).astype(BF16)

    @pl.when(j == 3)
    def _():
        k = _head_rms(z(), kg_ref[...], n_heads)
        k_ref[...] = k
        kb_ref[...] = k.astype(BF16)

    @pl.when(j == 4)
    def _():
        v = z()
        v_ref[...] = v
        vb_ref[...] = v.astype(BF16)


def _in_proj(x, g_mix, w_in_bf, q_gain, k_gain, n_heads, tm):
    t, d = x.shape
    c = w_in_bf.shape[1] // 5
    dh = c // n_heads
    tm = min(tm, t)
    assert t % tm == 0
    row = lambda i, j: (i, 0)
    const = lambda i, j: (0, 0)
    out_blk = pl.BlockSpec((tm, c), row)
    return pl.pallas_call(
        functools.partial(_in_proj_kernel, n_heads=n_heads),
        grid=(t // tm, 5),
        in_specs=[
            pl.BlockSpec((tm, d), row),
            pl.BlockSpec((1, d), const),
            pl.BlockSpec((d, c), lambda i, j: (0, j)),
            pl.BlockSpec((1, dh), const),
            pl.BlockSpec((1, dh), const),
        ],
        out_specs=[out_blk] * 7,
        out_shape=[SDS((t, c), F32), SDS((t, c), F32), SDS((t, c), BF16),
                   SDS((t, c), F32), SDS((t, c), F32), SDS((t, c), BF16),
                   SDS((t, c), BF16)],
        scratch_shapes=[pltpu.VMEM((tm, d), BF16)],
        compiler_params=_params(2),
        name="in_proj",
    )(x, g_mix, w_in_bf, q_gain, k_gain)


def _lru_gates(xc, wai_ref, ba, bi, lam):
    n_blocks, blk, _ = wai_ref.shape
    xcb = xc.astype(BF16)
    ga, gi = [], []
    for n in range(n_blocks):
        g = _dot(xcb[:, n * blk:(n + 1) * blk], wai_ref[n])
        ga.append(g[:, :blk])
        gi.append(g[:, blk:])
    r = jax.nn.sigmoid(jnp.concatenate(ga, axis=-1) + ba)
    ig = jax.nn.sigmoid(jnp.concatenate(gi, axis=-1) + bi)
    log_a = -LRU_C * r * _softplus(-lam)
    a = jnp.exp(log_a)
    th = jnp.tanh(log_a)
    u = jnp.sqrt(-2.0 * th / (1.0 - th)) * ig * xc
    return a, u


def _lru_prompt_kernel(xr_ref, gy_ref, st_ref, h0_ref, cw_ref, cb_ref, wai_ref,
                       ba_ref, bi_ref, lam_ref, gl_ref, mix_ref, ht_ref,
                       xext_ref, a_ref, u_ref, h_ref, *, tl):
    l = pl.program_id(1)

    @pl.when(l == 0)
    def _():
        xext_ref[0:8, :] = st_ref[0]
        h_ref[...] = h0_ref[0]

    xext_ref[8:8 + tl, :] = xr_ref[0]
    cw = cw_ref[...]
    n_taps = cw.shape[0]
    xc = cb_ref[...]
    for j in range(n_taps):
        off = 8 - (n_taps - 1) + j
        xc = xc + cw[j:j + 1, :] * xext_ref[off:off + tl, :]
    xext_ref[0:8, :] = xext_ref[tl:tl + 8, :]

    a, u = _lru_gates(xc, wai_ref, ba_ref[...], bi_ref[...], lam_ref[...])
    a_ref[...] = a
    u_ref[...] = u

    def step(t, h):
        h = a_ref[pl.ds(t, 1), :] * h + u_ref[pl.ds(t, 1), :]
        u_ref[pl.ds(t, 1), :] = h
        return h

    h = lax.fori_loop(0, tl, step, h_ref[...], unroll=8)
    h_ref[...] = h
    mix_ref[0] = _rms(u_ref[...] * gy_ref[0], gl_ref[...]).astype(BF16)

    @pl.when(l == pl.num_programs(1) - 1)
    def _():
        ht_ref[0] = h


def _lru_prompt(xr, gy, state8, h0, conv_w, conv_b, wai, b_a, b_i, lam, g_lru, tl):
    b, l, c = xr.shape
    tl = min(tl, l)
    assert l % tl == 0 and tl % 8 == 0
    tile = pl.BlockSpec((1, tl, c), lambda i, j: (i, j, 0))
    per_b = lambda rows: pl.BlockSpec((1, rows, c), lambda i, j: (i, 0, 0))
    vec = pl.BlockSpec((1, c), lambda i, j: (0, 0))
    return pl.pallas_call(
        functools.partial(_lru_prompt_kernel, tl=tl),
        grid=(b, l // tl),
        in_specs=[tile, tile, per_b(8), per_b(1),
                  pl.BlockSpec(conv_w.shape, lambda i, j: (0, 0)), vec,
                  pl.BlockSpec(wai.shape, lambda i, j: (0, 0, 0)),
                  vec, vec, vec, vec],
        out_specs=[tile, per_b(1)],
        out_shape=[SDS((b, l, c), BF16), SDS((b, 1, c), F32)],
        scratch_shapes=[pltpu.VMEM((tl + 8, c), F32), pltpu.VMEM((tl, c), F32),
                        pltpu.VMEM((tl, c), F32), pltpu.VMEM((1, c), F32)],
        compiler_params=_params(2),
        name="lru_prompt",
    )(xr, gy, state8, h0, conv_w, conv_b, wai, b_a, b_i, lam, g_lru)


def _lru_step_kernel(xr_ref, gy_ref, st_ref, h0_ref, cw_ref, cb_ref, wai_ref,
                     ba_ref, bi_ref, lam_ref, gl_ref, mix_ref, hn_ref):
    cw = cw_ref[...]
    n_taps = cw.shape[0]
    xc = cb_ref[...] + cw[n_taps - 1:n_taps, :] * xr_ref[...]
    for j in range(n_taps - 1):
        xc = xc + cw[j:j + 1, :] * st_ref[j]
    a, u = _lru_gates(xc, wai_ref, ba_ref[...], bi_ref[...], lam_ref[...])
    h = a * h0_ref[...] + u
    hn_ref[...] = h
    mix_ref[...] = _rms(h * gy_ref[...], gl_ref[...]).astype(BF16)


def _lru_step(xr, gy, state, h0, conv_w, conv_b, wai, b_a, b_i, lam, g_lru):
    b, c = xr.shape
    return pl.pallas_call(
        _lru_step_kernel,
        out_shape=[SDS((b, c), BF16), SDS((b, c), F32)],
        compiler_params=pltpu.CompilerParams(vmem_limit_bytes=VMEM_LIMIT_BYTES),
        name="lru_step",
    )(xr, gy, state, h0, conv_w, conv_b, wai, b_a, b_i, lam, g_lru)


def _attn_prompt_kernel(bias_ref, q_ref, k_ref, v_ref, tri_ref, o_ref, *, tq, tk, hp):
    hg = pl.program_id(1)
    i = pl.program_id(2)
    tri = tri_ref[...]
    dh = q_ref.shape[1] // hp
    r = tq // tk
    qs = [q_ref[:, a * dh:(a + 1) * dh] for a in range(hp)]
    biases = [bias_ref[hg * hp + a] for a in range(hp)]
    row = lax.broadcasted_iota(I32, (tq, tk), 0)
    col = lax.broadcasted_iota(I32, (tq, tk), 1)

    def tile(j, carry, diag):
        start = pl.multiple_of(j * tk, tk)
        new = []
        for a in range(hp):
            c, acc = carry[a]
            kj = k_ref[pl.ds(start, tk), a * dh:(a + 1) * dh]
            vj = v_ref[pl.ds(start, tk), a * dh:(a + 1) * dh]
            z = lax.dot_general(qs[a], kj, NT_DIMS, preferred_element_type=F32) + biases[a]
            sp = _softplus(z)
            log_keep = -sp
            if diag is not None:
                causal = col + diag * tk < row
                log_keep = jnp.where(causal, log_keep, 0.0)
            log_after = _dot(log_keep.astype(BF16), tri)
            w = jnp.exp(z - sp + log_after + c)
            if diag is not None:
                w = jnp.where(causal, w, 0.0)
            acc = acc + _dot(w.astype(BF16), vj)
            c = c + log_after[:, 0:1] + log_keep[:, 0:1]
            new.append((c, acc))
        return tuple(new)

    carry = tuple((jnp.zeros((tq, 1), F32), jnp.zeros((tq, dh), F32)) for _ in range(hp))
    for dj in reversed(range(r)):
        carry = tile(i * r + dj, carry, dj)
    carry = lax.fori_loop(0, i * r, lambda jj, cr: tile(i * r - 1 - jj, cr, None), carry)
    o_ref[...] = jnp.concatenate([carry[a][1] for a in range(hp)], axis=-1)


def _attn_prompt(q, k, v, sb_bias, batch, n_heads, tq, tk, hp):
    t, c = q.shape
    l = t // batch
    dh = c // n_heads
    tq = min(tq, l)
    tk = min(tk, tq)
    hp = min(hp, n_heads)
    assert l % tq == 0 and tq % tk == 0 and n_heads % hp == 0
    nq = l // tq
    tri = (jnp.arange(tk)[:, None] > jnp.arange(tk)[None, :]).astype(BF16)
    kv_spec = pl.BlockSpec((l, hp * dh), lambda b, h, i: (b, h))
    q_spec = pl.BlockSpec((tq, hp * dh), lambda b, h, i: (b * nq + i, h))
    return pl.pallas_call(
        functools.partial(_attn_prompt_kernel, tq=tq, tk=tk, hp=hp),
        grid=(batch, n_heads // hp, nq),
        in_specs=[pl.BlockSpec(memory_space=pltpu.SMEM), q_spec, kv_spec, kv_spec,
                  pl.BlockSpec((tk, tk), lambda b, h, i: (0, 0))],
        out_specs=q_spec,
        out_shape=SDS((t, c), F32),
        compiler_params=_params(3),
        name="attn_prompt",
    )(sb_bias, q, k, v, tri)


def _shift_lanes_left(x, s):
    n = x.shape[1]
    col = lax.broadcasted_iota(I32, x.shape, 1)
    return jnp.where(col < n - s, pltpu.roll(x, n - s, axis=1), 0.0)


def _shift_rows_down(x, s):
    rowi = lax.broadcasted_iota(I32, x.shape, 0)
    return jnp.where(rowi >= s, pltpu.roll(x, s, axis=0), 0.0)


def _attn_sample_kernel(pt_ref, bias_ref, q_ref, *refs, n_pages_step):
    del pt_ref
    g_ = n_pages_step
    k_refs = refs[:g_]
    v_refs = refs[g_:2 * g_]
    o_ref, c_ref, acc_ref = refs[2 * g_:]
    step = pl.program_id(1)

    @pl.when(step == 0)
    def _():
        c_ref[...] = jnp.zeros_like(c_ref)
        acc_ref[...] = jnp.zeros_like(acc_ref)

    q = q_ref[0]
    n_heads, dh = q.shape
    page = k_refs[0].shape[0]
    n_col = page * n_heads
    row = lax.broadcasted_iota(I32, (n_heads, n_col), 0)
    col = lax.broadcasted_iota(I32, (n_heads, n_col), 1)
    own = lax.rem(col, n_heads) == row
    pick = lambda t: jnp.sum(jnp.where(own, t, 0.0), axis=0, keepdims=True)
    bias_row = pick(jnp.broadcast_to(bias_ref[...], (n_heads, n_col)))
    prow = lax.broadcasted_iota(I32, (g_, n_col), 0)

    z = jnp.zeros((g_, n_col), F32)
    for g in range(g_):
        kp = k_refs[g][...].reshape(n_col, dh).astype(BF16)
        z8 = lax.dot_general(q, kp, NT_DIMS, preferred_element_type=F32)
        z = jnp.where(prow == g, jnp.broadcast_to(pick(z8), (g_, n_col)), z)
    z = z + bias_row
    sp = _softplus(z)
    log_keep = -sp
    log_after = _shift_lanes_left(log_keep, n_heads)
    total = log_keep
    s = n_heads
    while s < n_col:
        log_after = log_after + _shift_lanes_left(log_after, s)
        total = total + pltpu.roll(total, s, axis=1)
        s *= 2
    before = _shift_rows_down(total, 1)
    s = 1
    while s < g_:
        before = before + _shift_rows_down(before, s)
        s *= 2
    c_in = c_ref[...]
    w = jnp.exp(z - sp + log_after + before + c_in)
    c_ref[...] = c_in + jnp.sum(total, axis=0, keepdims=True)

    acc = acc_ref[...]
    for g in range(g_):
        w8 = jnp.where(own, jnp.broadcast_to(w[g:g + 1, :], (n_heads, n_col)), 0.0)
        vp = v_refs[g][...].reshape(n_col, dh).astype(BF16)
        acc = acc + _dot(w8.astype(BF16), vp)
    acc_ref[...] = acc

    @pl.when(step == pl.num_programs(1) - 1)
    def _():
        o_ref[0] = acc


def _attn_sample(q, cache_k, cache_v, page_table, sb_bias, n_pages_step):
    b, n_heads, dh = q.shape
    _, page, _, _ = cache_k.shape
    n_pages = page_table.shape[1]
    g_ = min(n_pages_step, n_pages)
    assert n_pages % g_ == 0 and page & (page - 1) == 0

    def page_spec(g):
        return pl.BlockSpec(
            (None, page, n_heads, dh),
            lambda i, j, pt: (pt[i, n_pages - 1 - (j * g_ + g)], 0, 0, 0))

    q_spec = pl.BlockSpec((1, n_heads, dh), lambda i, j, pt: (i, 0, 0))
    grid_spec = pltpu.PrefetchScalarGridSpec(
        num_scalar_prefetch=1,
        grid=(b, n_pages // g_),
        in_specs=[pl.BlockSpec((n_heads, 1), lambda i, j, pt: (0, 0)), q_spec]
        + [page_spec(g) for g in range(g_)] * 2,
        out_specs=q_spec,
        scratch_shapes=[pltpu.VMEM((1, page * n_heads), F32),
                        pltpu.VMEM((n_heads, dh), F32)],
    )
    return pl.pallas_call(
        functools.partial(_attn_sample_kernel, n_pages_step=g_),
        grid_spec=grid_spec,
        out_shape=SDS((b, n_heads, dh), F32),
        compiler_params=_params(2),
        name="attn_sample",
    )(page_table, sb_bias.reshape(n_heads, 1), q,
      *([cache_k] * g_), *([cache_v] * g_))


def _out_proj_kernel(ml_ref, att_ref, x_ref, ga_ref, wo_ref, gf_ref, wr_ref,
                     h_ref, hn_ref, lg_ref):
    c = ml_ref.shape[1]
    attn = _rms(att_ref[...], ga_ref[...]).astype(BF16)
    mix = _dot(ml_ref[...], wo_ref[0:c, :]) + _dot(attn, wo_ref[c:2 * c, :])
    h = x_ref[...] + mix
    h_ref[...] = h
    hn = _rms(h, gf_ref[...])
    hn_ref[...] = _pack_halves(hn)
    lg_ref[...] = lax.dot_general(wr_ref[...], hn, NT_DIMS,
                                  precision=lax.Precision.HIGHEST,
                                  preferred_element_type=F32)


def _out_proj(mix_lru, att, x, g_att, w_o_bf, g_ffn, w_router_t, tm):
    t, d = x.shape
    c = mix_lru.shape[1]
    e = w_router_t.shape[0]
    tm = min(tm, t)
    assert t % tm == 0
    row = lambda w: pl.BlockSpec((tm, w), lambda i: (i, 0))
    full = lambda a: pl.BlockSpec(a.shape, lambda i: (0, 0))
    return pl.pallas_call(
        _out_proj_kernel,
        grid=(t // tm,),
        in_specs=[row(c), row(c), row(d), full(g_att), full(w_o_bf), full(g_ffn),
                  full(w_router_t)],
        out_specs=[row(d), row(d // 2), pl.BlockSpec((e, tm), lambda i: (0, i))],
        out_shape=[SDS((t, d), F32), SDS((t, d // 2), U32), SDS((e, t), F32)],
        compiler_params=_params(1),
        name="out_proj",
    )(mix_lru, att, x, g_att, w_o_bf, g_ffn, w_router_t)


def _take_top(cur, idx, sentinel, n):
    picked = jnp.zeros(cur.shape, jnp.bool_)
    for _ in range(n):
        m = jnp.max(cur, axis=0, keepdims=True)
        first = jnp.min(jnp.where(cur == m, idx, sentinel), axis=0, keepdims=True)
        pick = idx == first
        picked = jnp.logical_or(picked, pick)
        cur = jnp.where(pick, -jnp.inf, cur)
    return picked


def _router_kernel(lg_ref, eb_ref, gate_ref):
    e, tm = lg_ref.shape
    per = e // N_GROUPS
    scores = jax.nn.sigmoid(lg_ref[...])
    biased = scores + eb_ref[...]
    sub = lax.broadcasted_iota(I32, (per, tm), 0)
    group_scores = []
    for g in range(N_GROUPS):
        bg = biased[g * per:(g + 1) * per, :]
        m1 = jnp.max(bg, axis=0, keepdims=True)
        first = jnp.min(jnp.where(bg == m1, sub, per), axis=0, keepdims=True)
        m2 = jnp.max(jnp.where(sub == first, -jnp.inf, bg), axis=0, keepdims=True)
        group_scores.append(m1 + m2)
    gs = jnp.concatenate(group_scores, axis=0)
    gidx = lax.broadcasted_iota(I32, gs.shape, 0)
    gsel = _take_top(gs, gidx, N_GROUPS, TOPK_GROUPS)
    emask = jnp.concatenate(
        [jnp.broadcast_to(gsel[g:g + 1, :], (per, tm)) for g in range(N_GROUPS)],
        axis=0)
    eidx = lax.broadcasted_iota(I32, (e, tm), 0)
    sel = _take_top(jnp.where(emask, biased, -jnp.inf), eidx, e, TOP_K)
    chosen = jnp.where(sel, scores, 0.0)
    gate_ref[...] = chosen / jnp.sum(chosen, axis=0, keepdims=True) * ROUTED_SCALE


def _router(logits_t, e_bias, tm):
    e, t = logits_t.shape
    tm = min(tm, t)
    assert t % tm == 0
    blk = pl.BlockSpec((e, tm), lambda i: (0, i))
    return pl.pallas_call(
        _router_kernel,
        grid=(t // tm,),
        in_specs=[blk, pl.BlockSpec((e, 1), lambda i: (0, 0))],
        out_specs=blk,
        out_shape=SDS((e, t), F32),
        compiler_params=_params(1),
        name="router",
    )(logits_t, e_bias.reshape(e, 1))


def _route_plan(gate_t, tm):
    e, t = gate_t.shape
    sel = gate_t > 0
    seli = sel.astype(I32)
    slot = jnp.cumsum(seli, axis=0) - seli
    rank = jnp.cumsum(seli, axis=1) - seli
    counts = jnp.sum(seli, axis=1)
    ends = jnp.cumsum(counts)
    offsets = ends - counts
    pos_e = offsets[:, None] + rank
    hit = sel[:, :, None] & (slot[:, :, None] == jnp.arange(TOP_K)[None, None, :])
    pos_tk = jnp.sum(jnp.where(hit, pos_e[:, :, None], 0), axis=0)
    w_tk = jnp.sum(jnp.where(hit, gate_t[:, :, None], 0.0), axis=0)

    n_tiles = pl.cdiv(t * TOP_K, tm)
    n_work = n_tiles + e
    first_tile = offsets // tm
    n_e = jnp.where(counts > 0, (ends - 1) // tm - first_tile + 1, 0)
    w_end = jnp.cumsum(n_e)
    w_start = w_end - n_e
    total = w_end[-1]
    w = jnp.arange(n_work, dtype=I32)
    valid = w < total
    wc = jnp.minimum(w, total - 1)
    ex = jnp.minimum(jnp.searchsorted(w_end, wc, side="right"), e - 1).astype(I32)
    tile = first_tile[ex] + (wc - w_start[ex])
    lo = jnp.where(valid, jnp.maximum(offsets[ex], tile * tm) - tile * tm, 0)
    hi = jnp.where(valid, jnp.minimum(ends[ex], (tile + 1) * tm) - tile * tm, 0)
    prev_tile = jnp.concatenate([jnp.full((1,), -1, I32), tile[:-1]])
    next_tile = jnp.concatenate([tile[1:], jnp.full((1,), -1, I32)])
    prev_ex = jnp.concatenate([jnp.full((1,), -1, I32), ex[:-1]])
    first = valid & (tile != prev_tile)
    last = valid & ((tile != next_tile) | (w == total - 1))
    new_ex = valid & (ex != prev_ex)
    flags = first.astype(I32) + 2 * last.astype(I32) + 4 * new_ex.astype(I32)
    work = (tile.astype(I32), ex, lo.astype(I32), hi.astype(I32), flags)
    return pos_tk.astype(I32), w_tk, work, n_tiles


def _dispatch_kernel(pos_ref, xa_ref, xb_ref, xs_ref, stage_ref, sem_ref, *, k):
    tt = xa_ref.shape[0]
    tb = xb_ref.shape[0]
    s = pl.program_id(0)
    n = pl.num_programs(0)
    slot = lax.rem(s, 2)

    def wait_slot(sl, rows):
        for _ in range(k):
            pltpu.make_async_copy(stage_ref.at[sl, pl.ds(0, rows)], xs_ref.at[pl.ds(0, rows)],
                                  sem_ref.at[sl]).wait()

    def issue(rows):
        def body(r, carry):
            for kk in range(k):
                dst = pos_ref[0, 0, r * k + kk]
                pltpu.make_async_copy(stage_ref.at[slot, pl.ds(r, 1)],
                                      xs_ref.at[pl.ds(dst, 1)], sem_ref.at[slot]).start()
            return carry
        lax.fori_loop(0, rows, body, 0)

    @pl.when(s >= 2)
    def _():
        wait_slot(slot, tt)

    @pl.when(s < n - 1)
    def _():
        stage_ref[slot] = xa_ref[...]
        issue(tt)

    @pl.when(s == n - 1)
    def _():
        stage_ref[slot, 0:tb, :] = xb_ref[...]
        issue(tb)
        wait_slot(slot, tb)
        wait_slot(1 - slot, tt)


def _dispatch(xa, xb, pos_tk, n_rows, tt):
    ta, dw = xa.shape
    tb = xb.shape[0]
    k = pos_tk.shape[1]
    tt = min(tt, ta)
    assert ta % tt == 0 and tb <= tt
    n_a = ta // tt
    pos_b = jnp.zeros((tt * k,), I32).at[:tb * k].set(pos_tk[ta:].reshape(-1))
    pos = jnp.concatenate([pos_tk[:ta].reshape(-1), pos_b]).reshape(n_a + 1, 1, tt * k)
    return pl.pallas_call(
        functools.partial(_dispatch_kernel, k=k),
        grid=(n_a + 1,),
        in_specs=[pl.BlockSpec((1, 1, tt * k), lambda s: (s, 0, 0), memory_space=pltpu.SMEM),
                  pl.BlockSpec((tt, dw), lambda s: (jnp.minimum(s, n_a - 1), 0)),
                  pl.BlockSpec((tb, dw), lambda s: (0, 0))],
        out_specs=pl.BlockSpec(memory_space=pl.ANY),
        out_shape=SDS((n_rows, dw), U32),
        scratch_shapes=[pltpu.VMEM((2, tt, dw), U32), pltpu.SemaphoreType.DMA((2,))],
        compiler_params=_params(1),
        name="dispatch",
    )(pos, xa, xb)


def _moe_kernel(tile_ref, ex_ref, lo_ref, hi_ref, flag_ref,
                xs_ref, wg_ref, wu_ref, wd_ref, ys_ref,
                wgb_ref, wub_ref, wdb_ref, acc_ref):
    del tile_ref, ex_ref
    w = pl.program_id(0)
    lo = lo_ref[w]
    hi = hi_ref[w]
    flags = flag_ref[w]

    @pl.when((flags & 4) != 0)
    def _():
        wgb_ref[...] = wg_ref[...].astype(BF16)
        wub_ref[...] = wu_ref[...].astype(BF16)
        wdb_ref[...] = wd_ref[...].astype(BF16)

    @pl.when(hi > lo)
    def _():
        x_lo, x_hi = _unpack_halves(xs_ref[...])
        x_lo = x_lo.astype(BF16)
        x_hi = x_hi.astype(BF16)
        half = x_lo.shape[1]
        hg = _dot(x_lo, wgb_ref[0:half, :]) + _dot(x_hi, wgb_ref[half:2 * half, :])
        hu = _dot(x_lo, wub_ref[0:half, :]) + _dot(x_hi, wub_ref[half:2 * half, :])
        rowi = lax.broadcasted_iota(I32, hg.shape, 0)
        mine = jnp.logical_and(rowi >= lo, rowi < hi)
        act = jnp.where(mine, _silu(hg) * hu, 0.0)
        y = _dot(act.astype(BF16), wdb_ref[...])

        @pl.when((flags & 1) != 0)
        def _():
            acc_ref[...] = y

        @pl.when((flags & 1) == 0)
        def _():
            acc_ref[...] += y

    @pl.when((flags & 2) != 0)
    def _():
        ys_ref[...] = _pack_halves(acc_ref[...])


def _moe(xs, work, w_eg, w_eu, w_ed, tm):
    n_rows, dw = xs.shape
    e, d, f = w_eg.shape
    n_work = work[0].shape[0]
    row_spec = pl.BlockSpec((tm, dw), lambda w, tile, ex, lo, hi, fl: (tile[w], 0))
    up_spec = pl.BlockSpec((None, d, f), lambda w, tile, ex, lo, hi, fl: (ex[w], 0, 0))
    dn_spec = pl.BlockSpec((None, f, d), lambda w, tile, ex, lo, hi, fl: (ex[w], 0, 0))
    grid_spec = pltpu.PrefetchScalarGridSpec(
        num_scalar_prefetch=5,
        grid=(n_work,),
        in_specs=[row_spec, up_spec, up_spec, dn_spec],
        out_specs=row_spec,
        scratch_shapes=[pltpu.VMEM((d, f), BF16), pltpu.VMEM((d, f), BF16),
                        pltpu.VMEM((f, d), BF16), pltpu.VMEM((tm, d), F32)],
    )
    return pl.pallas_call(
        _moe_kernel,
        grid_spec=grid_spec,
        out_shape=SDS((n_rows, dw), U32),
        compiler_params=_params(1),
        name="moe",
    )(*work, xs, w_eg, w_eu, w_ed)


def _final_kernel(pos_ref, posn_ref, w_ref, h_ref, hn_ref, p_ref, ys_ref,
                  wsg_ref, wsu_ref, wsd_ref, gp_ref, wpg_ref, wpp_ref, y_ref,
                  gbuf_ref, sem_ref, *, tt, k):
    s = pl.program_id(0)
    n = pl.num_programs(0)
    slot = lax.rem(s, 2)

    def issue(idx_ref, sl):
        def body(r, carry):
            for kk in range(k):
                src = idx_ref[0, 0, r * k + kk]
                pltpu.make_async_copy(ys_ref.at[pl.ds(src, 1)],
                                      gbuf_ref.at[sl, kk, pl.ds(r, 1)],
                                      sem_ref.at[sl]).start()
            return carry
        lax.fori_loop(0, tt, body, 0)

    @pl.when(s == 0)
    def _():
        issue(pos_ref, 0)

    @pl.when(s + 1 < n)
    def _():
        issue(posn_ref, 1 - slot)

    for kk in range(k):
        pltpu.make_async_copy(ys_ref.at[pl.ds(0, tt)], gbuf_ref.at[slot, kk],
                              sem_ref.at[slot]).wait()

    wts = w_ref[...]
    r_lo = None
    for kk in range(k):
        lo, hi = _unpack_halves(gbuf_ref[slot, kk])
        wk = wts[:, kk:kk + 1]
        r_lo = wk * lo if r_lo is None else r_lo + wk * lo
        r_hi = wk * hi if kk == 0 else r_hi + wk * hi
    routed = jnp.concatenate([r_lo, r_hi], axis=-1)

    n_lo, n_hi = _unpack_halves(hn_ref[...])
    n_lo = n_lo.astype(BF16)
    n_hi = n_hi.astype(BF16)
    half = n_lo.shape[1]
    sg = _dot(n_lo, wsg_ref[0:half, :]) + _dot(n_hi, wsg_ref[half:2 * half, :])
    su = _dot(n_lo, wsu_ref[0:half, :]) + _dot(n_hi, wsu_ref[half:2 * half, :])
    shared = _dot((_silu(sg) * su).astype(BF16), wsd_ref[...])
    h2 = h_ref[...] + routed + shared
    gate = jax.nn.sigmoid(_dot(_rms(h2, gp_ref[...]).astype(BF16), wpg_ref[...]))
    y_ref[...] = h2 + gate * _dot(p_ref[...].astype(BF16), wpp_ref[...])


def _final(h, hn_packed, p, ys, pos_tk, w_tk, w_sg, w_su, w_sd, g_ple, w_pg, w_pp, tt):
    t, d = h.shape
    k = pos_tk.shape[1]
    dw = ys.shape[1]
    tt = min(tt, t)
    assert t % tt == 0
    n = t // tt
    pos = pos_tk.reshape(n, 1, tt * k)
    row = lambda w: pl.BlockSpec((tt, w), lambda s: (s, 0))
    const = lambda a: pl.BlockSpec(a.shape, lambda s: (0, 0), pipeline_mode=pl.Buffered(1))
    smem = lambda fn: pl.BlockSpec((1, 1, tt * k), fn, memory_space=pltpu.SMEM)
    return pl.pallas_call(
        functools.partial(_final_kernel, tt=tt, k=k),
        grid=(n,),
        in_specs=[smem(lambda s: (s, 0, 0)),
                  smem(lambda s: (jnp.minimum(s + 1, n - 1), 0, 0)),
                  row(k), row(d), row(d // 2), row(p.shape[1]),
                  pl.BlockSpec(memory_space=pl.ANY),
                  const(w_sg), const(w_su), const(w_sd), const(g_ple), const(w_pg),
                  const(w_pp)],
        out_specs=row(d),
        out_shape=SDS((t, d), F32),
        scratch_shapes=[pltpu.VMEM((2, k, tt, dw), U32), pltpu.SemaphoreType.DMA((2,))],
        compiler_params=_params(1),
        name="final",
    )(pos, pos, w_tk, h, hn_packed, p, ys, w_sg, w_su, w_sd, g_ple, w_pg, w_pp)


MOE_TILE = 256


def kernel(x_prompt, x_sample, p_prompt, p_sample, cache_k, cache_v, state_conv, state_h, page_table, g_mix, w_in, conv_w, conv_b, w_a, b_a, w_i, b_i, lru_lambda, q_gain, k_gain, sb_bias, g_lru_out, g_att_out, w_o, g_ffn, w_router, e_bias, w_eg, w_eu, w_ed, w_sg, w_su, w_sd, g_ple, w_ple_gate, w_ple_proj):
    depth = g_mix.shape[0]
    assert depth == 1
    batch, seq, d = x_prompt.shape
    dec_batch, dec_seq, _ = x_sample.shape
    assert dec_seq == 1
    n_heads, dh = cache_k.shape[3], cache_k.shape[4]
    c = n_heads * dh
    n_taps = conv_w.shape[1]
    assert w_in.shape[2] == 5 * c and seq >= n_taps - 1
    t_p = batch * seq

    row = lambda a: a.reshape(1, -1)
    wai = jnp.concatenate([w_a[0], w_i[0]], axis=-1).astype(BF16)
    lru_w = (conv_w[0], row(conv_b[0]), wai, row(b_a[0]), row(b_i[0]),
             row(lru_lambda[0]), row(g_lru_out[0]))
    in_proj = functools.partial(_in_proj, g_mix=row(g_mix[0]), w_in_bf=w_in[0].astype(BF16),
                                q_gain=row(q_gain[0]), k_gain=row(k_gain[0]),
                                n_heads=n_heads)
    out_proj = functools.partial(_out_proj, g_att=row(g_att_out[0]), w_o_bf=w_o[0].astype(BF16),
                                 g_ffn=row(g_ffn[0]), w_router_t=w_router[0].T)
    final = functools.partial(
        _final, w_sg=w_sg[0].astype(BF16), w_su=w_su[0].astype(BF16),
        w_sd=w_sd[0].astype(BF16), g_ple=row(g_ple[0]), w_pg=w_ple_gate[0].astype(BF16),
        w_pp=w_ple_proj[0].astype(BF16))

    xp = x_prompt.reshape(t_p, d)
    xr, gy, q, k, v, kb, vb = in_proj(xp, tm=512)
    mix_lru, h_last = _lru_prompt(
        xr.reshape(batch, seq, c), gy.reshape(batch, seq, c),
        jnp.zeros((batch, 8, c), F32), jnp.zeros((batch, 1, c), F32), *lru_w, tl=256)
    att = _attn_prompt(q, kb, vb, sb_bias[0], batch, n_heads, tq=512, tk=256, hp=2)
    h_p, hn_p, logits_p = out_proj(mix_lru.reshape(t_p, c), att, xp, tm=256)

    xs_ = x_sample.reshape(dec_batch, d)
    xr_s, gy_s, q_s, k_s, v_s, _, _ = in_proj(xs_, tm=dec_batch)
    mix_lru_s, h_s = _lru_step(xr_s, gy_s, jnp.swapaxes(state_conv[0], 0, 1),
                               state_h[0], *lru_w)
    att_s = _attn_sample(q_s.reshape(dec_batch, n_heads, dh), cache_k[0], cache_v[0],
                         page_table, sb_bias[0], n_pages_step=8)
    h_s2, hn_s, logits_s = out_proj(mix_lru_s, att_s.reshape(dec_batch, c), xs_, tm=dec_batch)

    gate_t = jnp.concatenate([_router(logits_p, e_bias[0], tm=256),
                              _router(logits_s, e_bias[0], tm=256)], axis=1)
    pos_tk, w_tk, work, n_tiles = _route_plan(gate_t, MOE_TILE)
    n_rows = n_tiles * MOE_TILE
    rows = _dispatch(hn_p, hn_s, pos_tk, n_rows, tt=128)
    ys = _moe(rows, work, w_eg[0], w_eu[0], w_ed[0], MOE_TILE)
    y_prompt = final(h_p, hn_p, p_prompt[0].reshape(t_p, -1), ys, pos_tk[:t_p], w_tk[:t_p],
                     tt=256)
    y_sample = final(h_s2, hn_s, p_sample[0].reshape(dec_batch, -1), ys, pos_tk[t_p:],
                     w_tk[t_p:], tt=256)

    k_prompt = k.reshape(1, batch, seq, n_heads, dh)
    v_prompt = v.reshape(1, batch, seq, n_heads, dh)
    conv_prompt = xr.reshape(batch, seq, c)[:, seq - (n_taps - 1):][None]
    h_prompt = h_last.reshape(1, batch, c)
    k_sample = k_s.reshape(1, dec_batch, 1, n_heads, dh)
    v_sample = v_s.reshape(1, dec_batch, 1, n_heads, dh)
    conv_sample = jnp.concatenate([state_conv[0][:, 1:], xr_s[:, None]], axis=1)[None]
    h_sample = h_s.reshape(1, dec_batch, c)

    return (y_prompt.reshape(batch, seq, d), y_sample.reshape(dec_batch, 1, d),
            k_prompt, v_prompt, conv_prompt, h_prompt,
            k_sample, v_sample, conv_sample, h_sample)
```

```python
import functools

import jax
import jax.numpy as jnp
from jax import lax
from jax.experimental import pallas as pl
from jax.experimental.pallas import tpu as pltpu

F32 = jnp.float32
BF16 = jnp.bfloat16
U32 = jnp.uint32
I32 = jnp.int32
SDS = jax.ShapeDtypeStruct

EPS = 1e-6
LRU_C = 8.0
N_GROUPS = 8
TOPK_GROUPS = 4
TOP_K = 8
ROUTED_SCALE = 2.5

VMEM_LIMIT_BYTES = 56 * 1024 * 1024

NT_DIMS = (((1,), (1,)), ((), ()))


def _params(n_grid_axes):
    return pltpu.CompilerParams(
        dimension_semantics=("arbitrary",) * n_grid_axes,
        vmem_limit_bytes=VMEM_LIMIT_BYTES,
    )


def _rms(x, g):
    return x * lax.rsqrt(jnp.mean(x * x, axis=-1, keepdims=True) + EPS) * g


def _softplus(x):
    return jnp.maximum(x, 0.0) + jnp.log(1.0 + jnp.exp(-jnp.abs(x)))


def _silu(x):
    return x * jax.nn.sigmoid(x)


def _dot(a, b):
    return jnp.dot(a, b, preferred_element_type=F32)


def _pack_halves(x):
    half = x.shape[1] // 2
    lo = lax.bitcast_convert_type(x[:, :half].astype(BF16).astype(F32), U32)
    hi = lax.bitcast_convert_type(x[:, half:].astype(BF16).astype(F32), U32)
    return (lo >> 16) | hi


LANES = 128


def _store_token_rows(ref, packed):
    m, w = packed.shape
    ns = w // LANES
    for s in range(ns):
        ref[pl.ds(s, m, stride=ns), :] = packed[:, s * LANES:(s + 1) * LANES]


def _load_token_rows(ref, m):
    ns = ref.shape[0] // m
    return jnp.concatenate([ref[pl.ds(s, m, stride=ns), :] for s in range(ns)], axis=-1)


def _unpack_halves(w):
    lo = lax.bitcast_convert_type(w << 16, F32)
    hi = lax.bitcast_convert_type(w & jnp.uint32(0xFFFF0000), F32)
    return lo, hi


def _head_rms(z, gain, n_heads):
    dh = z.shape[1] // n_heads
    return jnp.concatenate(
        [_rms(z[:, h * dh:(h + 1) * dh], gain) for h in range(n_heads)], axis=-1)


def _in_proj_kernel(x_ref, g_ref, w_ref, qg_ref, kg_ref,
                    xr_ref, gy_ref, q_ref, k_ref, v_ref, kb_ref, vb_ref,
                    xn_ref, *, n_heads):
    j = pl.program_id(1)

    @pl.when(j == 0)
    def _():
        xn_ref[...] = _rms(x_ref[...], g_ref[...]).astype(BF16)

    def z():
        return _dot(xn_ref[...], w_ref[...])

    @pl.when(j == 0)
    def _():
        xr_ref[...] = z()

    @pl.when(j == 1)
    def _():
        gy_ref[...] = jax.nn.gelu(z())

    @pl.when(j == 2)
    def _():
        dh = q_ref.shape[1] // n_heads
        q_ref[...] = (_head_rms(z(), qg_ref[...], n_heads) * dh ** -0.5).astype(BF16)

    @pl.when(j == 3)
    def _():
        k = _head_rms(z(), kg_ref[...], n_heads)
        k_ref[...] = k
        kb_ref[...] = k.astype(BF16)

    @pl.when(j == 4)
    def _():
        v = z()
        v_ref[...] = v
        vb_ref[...] = v.astype(BF16)


def _in_proj(x, g_mix, w_in_bf, q_gain, k_gain, n_heads, tm):
    t, d = x.shape
    c = w_in_bf.shape[1] // 5
    dh = c // n_heads
    tm = min(tm, t)
    assert t % tm == 0
    row = lambda i, j: (i, 0)
    const = lambda i, j: (0, 0)
    out_blk = pl.BlockSpec((tm, c), row)
    return pl.pallas_call(
        functools.partial(_in_proj_kernel, n_heads=n_heads),
        grid=(t // tm, 5),
        in_specs=[
            pl.BlockSpec((tm, d), row),
            pl.BlockSpec((1, d), const),
            pl.BlockSpec((d, c), lambda i, j: (0, j)),
            pl.BlockSpec((1, dh), const),
            pl.BlockSpec((1, dh), const),
        ],
        out_specs=[out_blk] * 7,
        out_shape=[SDS((t, c), F32), SDS((t, c), F32), SDS((t, c), BF16),
                   SDS((t, c), F32), SDS((t, c), F32), SDS((t, c), BF16),
                   SDS((t, c), BF16)],
        scratch_shapes=[pltpu.VMEM((tm, d), BF16)],
        compiler_params=_params(2),
        name="in_proj",
    )(x, g_mix, w_in_bf, q_gain, k_gain)


def _lru_gates(xc, wai_ref, ba, bi, lam):
    n_blocks, blk, _ = wai_ref.shape
    xcb = xc.astype(BF16)
    ga, gi = [], []
    for n in range(n_blocks):
        g = _dot(xcb[:, n * blk:(n + 1) * blk], wai_ref[n])
        ga.append(g[:, :blk])
        gi.append(g[:, blk:])
    r = jax.nn.sigmoid(jnp.concatenate(ga, axis=-1) + ba)
    ig = jax.nn.sigmoid(jnp.concatenate(gi, axis=-1) + bi)
    log_a = -LRU_C * r * _softplus(-lam)
    a = jnp.exp(log_a)
    th = jnp.tanh(log_a)
    u = jnp.sqrt(-2.0 * th / (1.0 - th)) * ig * xc
    return a, u


def _lru_prompt_kernel(xr_ref, gy_ref, st_ref, h0_ref, cw_ref, cb_ref, wai_ref,
                       ba_ref, bi_ref, lam_ref, gl_ref, mix_ref, ht_ref,
                       xext_ref, a_ref, u_ref, h_ref, *, tl):
    l = pl.program_id(1)

    @pl.when(l == 0)
    def _():
        xext_ref[0:8, :] = st_ref[0]
        h_ref[...] = h0_ref[0]

    xext_ref[8:8 + tl, :] = xr_ref[0]
    cw = cw_ref[...]
    n_taps = cw.shape[0]
    xc = cb_ref[...]
    for j in range(n_taps):
        off = 8 - (n_taps - 1) + j
        xc = xc + cw[j:j + 1, :] * xext_ref[off:off + tl, :]
    xext_ref[0:8, :] = xext_ref[tl:tl + 8, :]

    a, u = _lru_gates(xc, wai_ref, ba_ref[...], bi_ref[...], lam_ref[...])
    a_ref[...] = a
    u_ref[...] = u

    def step(t, h):
        h = a_ref[pl.ds(t, 1), :] * h + u_ref[pl.ds(t, 1), :]
        u_ref[pl.ds(t, 1), :] = h
        return h

    h = lax.fori_loop(0, tl, step, h_ref[...], unroll=8)
    h_ref[...] = h
    mix_ref[0] = _rms(u_ref[...] * gy_ref[0], gl_ref[...]).astype(BF16)

    @pl.when(l == pl.num_programs(1) - 1)
    def _():
        ht_ref[0] = h


def _lru_prompt(xr, gy, state8, h0, conv_w, conv_b, wai, b_a, b_i, lam, g_lru, tl):
    b, l, c = xr.shape
    tl = min(tl, l)
    assert l % tl == 0 and tl % 8 == 0
    tile = pl.BlockSpec((1, tl, c), lambda i, j: (i, j, 0))
    per_b = lambda rows: pl.BlockSpec((1, rows, c), lambda i, j: (i, 0, 0))
    vec = pl.BlockSpec((1, c), lambda i, j: (0, 0))
    return pl.pallas_call(
        functools.partial(_lru_prompt_kernel, tl=tl),
        grid=(b, l // tl),
        in_specs=[tile, tile, per_b(8), per_b(1),
                  pl.BlockSpec(conv_w.shape, lambda i, j: (0, 0)), vec,
                  pl.BlockSpec(wai.shape, lambda i, j: (0, 0, 0)),
                  vec, vec, vec, vec],
        out_specs=[tile, per_b(1)],
        out_shape=[SDS((b, l, c), BF16), SDS((b, 1, c), F32)],
        scratch_shapes=[pltpu.VMEM((tl + 8, c), F32), pltpu.VMEM((tl, c), F32),
                        pltpu.VMEM((tl, c), F32), pltpu.VMEM((1, c), F32)],
        compiler_params=_params(2),
        name="lru_prompt",
    )(xr, gy, state8, h0, conv_w, conv_b, wai, b_a, b_i, lam, g_lru)


def _lru_step_kernel(xr_ref, gy_ref, st_ref, h0_ref, cw_ref, cb_ref, wai_ref,
                     ba_ref, bi_ref, lam_ref, gl_ref, mix_ref, hn_ref):
    cw = cw_ref[...]
    n_taps = cw.shape[0]
    xc = cb_ref[...] + cw[n_taps - 1:n_taps, :] * xr_ref[...]
    for j in range(n_taps - 1):
        xc = xc + cw[j:j + 1, :] * st_ref[j]
    a, u = _lru_gates(xc, wai_ref, ba_ref[...], bi_ref[...], lam_ref[...])
    h = a * h0_ref[...] + u
    hn_ref[...] = h
    mix_ref[...] = _rms(h * gy_ref[...], gl_ref[...]).astype(BF16)


def _lru_step(xr, gy, state, h0, conv_w, conv_b, wai, b_a, b_i, lam, g_lru):
    b, c = xr.shape
    return pl.pallas_call(
        _lru_step_kernel,
        out_shape=[SDS((b, c), BF16), SDS((b, c), F32)],
        compiler_params=pltpu.CompilerParams(vmem_limit_bytes=VMEM_LIMIT_BYTES),
        name="lru_step",
    )(xr, gy, state, h0, conv_w, conv_b, wai, b_a, b_i, lam, g_lru)


def _attn_prompt_kernel(bias_ref, q_ref, k_ref, v_ref, tri_ref, o_ref, *, tq, tk, hp):
    hg = pl.program_id(1)
    i = pl.program_id(2)
    tri = tri_ref[...]
    dh = q_ref.shape[1] // hp
    r = tq // tk
    qs = [q_ref[:, a * dh:(a + 1) * dh] for a in range(hp)]
    biases = [bias_ref[hg * hp + a] for a in range(hp)]
    row = lax.broadcasted_iota(I32, (tq, tk), 0)
    col = lax.broadcasted_iota(I32, (tq, tk), 1)

    def tile(j, carry, diag):
        start = pl.multiple_of(j * tk, tk)
        new = []
        for a in range(hp):
            c, acc = carry[a]
            kj = k_ref[pl.ds(start, tk), a * dh:(a + 1) * dh]
            vj = v_ref[pl.ds(start, tk), a * dh:(a + 1) * dh]
            z = lax.dot_general(qs[a], kj, NT_DIMS, preferred_element_type=F32) + biases[a]
            neg_abs = lax.bitcast_convert_type(
                lax.bitcast_convert_type(z, U32) | jnp.uint32(0x80000000), F32)
            sp = jnp.maximum(z, 0.0) + jnp.log(1.0 + jnp.exp(neg_abs))
            if diag is not None:
                causal = col + diag * tk < row
                sp = jnp.where(causal, sp, 0.0)
            drop = _dot(sp.astype(BF16), tri)
            w = jnp.exp(z - c - drop)
            if diag is not None:
                w = jnp.where(causal, w, 0.0)
            acc = acc + _dot(w.astype(BF16), vj)
            c = c + drop[:, 0:1]
            new.append((c, acc))
        return tuple(new)

    carry = tuple((jnp.zeros((tq, 1), F32), jnp.zeros((tq, dh), F32)) for _ in range(hp))
    for dj in reversed(range(r)):
        carry = tile(i * r + dj, carry, dj)
    carry = lax.fori_loop(0, i * r, lambda jj, cr: tile(i * r - 1 - jj, cr, None), carry)
    o_ref[...] = jnp.concatenate([carry[a][1] for a in range(hp)], axis=-1)


def _attn_prompt(q, k, v, sb_bias, batch, n_heads, tq, tk, hp):
    t, c = q.shape
    l = t // batch
    dh = c // n_heads
    tq = min(tq, l)
    tk = min(tk, tq)
    hp = min(hp, n_heads)
    assert l % tq == 0 and tq % tk == 0 and n_heads % hp == 0
    nq = l // tq
    tri = (jnp.arange(tk)[:, None] >= jnp.arange(tk)[None, :]).astype(BF16)
    kv_spec = pl.BlockSpec((l, hp * dh), lambda b, h, i: (b, h))
    q_spec = pl.BlockSpec((tq, hp * dh), lambda b, h, i: (b * nq + i, h))
    return pl.pallas_call(
        functools.partial(_attn_prompt_kernel, tq=tq, tk=tk, hp=hp),
        grid=(batch, n_heads // hp, nq),
        in_specs=[pl.BlockSpec(memory_space=pltpu.SMEM), q_spec, kv_spec, kv_spec,
                  pl.BlockSpec((tk, tk), lambda b, h, i: (0, 0))],
        out_specs=q_spec,
        out_shape=SDS((t, c), F32),
        compiler_params=_params(3),
        name="attn_prompt",
    )(sb_bias, q, k, v, tri)


def _shift_lanes_left(x, s):
    n = x.shape[1]
    col = lax.broadcasted_iota(I32, x.shape, 1)
    return jnp.where(col < n - s, pltpu.roll(x, n - s, axis=1), 0.0)


def _shift_rows_down(x, s):
    rowi = lax.broadcasted_iota(I32, x.shape, 0)
    return jnp.where(rowi >= s, pltpu.roll(x, s, axis=0), 0.0)


def _attn_sample_kernel(pt_ref, bias_ref, q_ref, *refs, n_pages_step):
    del pt_ref
    g_ = n_pages_step
    k_refs = refs[:g_]
    v_refs = refs[g_:2 * g_]
    o_ref, c_ref, acc_ref = refs[2 * g_:]
    step = pl.program_id(1)

    @pl.when(step == 0)
    def _():
        c_ref[...] = jnp.zeros_like(c_ref)
        acc_ref[...] = jnp.zeros_like(acc_ref)

    q = q_ref[0]
    n_heads, dh = q.shape
    page = k_refs[0].shape[0]
    n_col = page * n_heads
    row = lax.broadcasted_iota(I32, (n_heads, n_col), 0)
    col = lax.broadcasted_iota(I32, (n_heads, n_col), 1)
    own = lax.rem(col, n_heads) == row
    pick = lambda t: jnp.sum(jnp.where(own, t, 0.0), axis=0, keepdims=True)
    bias_row = pick(jnp.broadcast_to(bias_ref[...], (n_heads, n_col)))
    prow = lax.broadcasted_iota(I32, (g_, n_col), 0)

    z = jnp.zeros((g_, n_col), F32)
    for g in range(g_):
        kp = k_refs[g][...].reshape(n_col, dh).astype(BF16)
        z8 = lax.dot_general(q, kp, NT_DIMS, preferred_element_type=F32)
        z = jnp.where(prow == g, jnp.broadcast_to(pick(z8), (g_, n_col)), z)
    z = z + bias_row
    sp = _softplus(z)
    log_keep = -sp
    log_after = _shift_lanes_left(log_keep, n_heads)
    total = log_keep
    s = n_heads
    while s < n_col:
        log_after = log_after + _shift_lanes_left(log_after, s)
        total = total + pltpu.roll(total, s, axis=1)
        s *= 2
    before = _shift_rows_down(total, 1)
    s = 1
    while s < g_:
        before = before + _shift_rows_down(before, s)
        s *= 2
    c_in = c_ref[...]
    w = jnp.exp(z - sp + log_after + before + c_in)
    c_ref[...] = c_in + jnp.sum(total, axis=0, keepdims=True)

    acc = acc_ref[...]
    for g in range(g_):
        w8 = jnp.where(own, jnp.broadcast_to(w[g:g + 1, :], (n_heads, n_col)), 0.0)
        vp = v_refs[g][...].reshape(n_col, dh).astype(BF16)
        acc = acc + _dot(w8.astype(BF16), vp)
    acc_ref[...] = acc

    @pl.when(step == pl.num_programs(1) - 1)
    def _():
        o_ref[0] = acc


def _attn_sample(q, cache_k, cache_v, page_table, sb_bias, n_pages_step):
    b, n_heads, dh = q.shape
    _, page, _, _ = cache_k.shape
    n_pages = page_table.shape[1]
    g_ = min(n_pages_step, n_pages)
    assert n_pages % g_ == 0 and page & (page - 1) == 0

    def page_spec(g):
        return pl.BlockSpec(
            (None, page, n_heads, dh),
            lambda i, j, pt: (pt[i, n_pages - 1 - (j * g_ + g)], 0, 0, 0))

    q_spec = pl.BlockSpec((1, n_heads, dh), lambda i, j, pt: (i, 0, 0))
    grid_spec = pltpu.PrefetchScalarGridSpec(
        num_scalar_prefetch=1,
        grid=(b, n_pages // g_),
        in_specs=[pl.BlockSpec((n_heads, 1), lambda i, j, pt: (0, 0)), q_spec]
        + [page_spec(g) for g in range(g_)] * 2,
        out_specs=q_spec,
        scratch_shapes=[pltpu.VMEM((1, page * n_heads), F32),
                        pltpu.VMEM((n_heads, dh), F32)],
    )
    return pl.pallas_call(
        functools.partial(_attn_sample_kernel, n_pages_step=g_),
        grid_spec=grid_spec,
        out_shape=SDS((b, n_heads, dh), F32),
        compiler_params=_params(2),
        name="attn_sample",
    )(page_table, sb_bias.reshape(n_heads, 1), q,
      *([cache_k] * g_), *([cache_v] * g_))


def _out_proj_kernel(ml_ref, att_ref, x_ref, ga_ref, wo_ref, gf_ref, wr_ref,
                     h_ref, hn_ref, lg_ref):
    c = ml_ref.shape[1]
    attn = _rms(att_ref[...], ga_ref[...]).astype(BF16)
    mix = _dot(ml_ref[...], wo_ref[0:c, :]) + _dot(attn, wo_ref[c:2 * c, :])
    h = x_ref[...] + mix
    h_ref[...] = h
    hn = _rms(h, gf_ref[...])
    _store_token_rows(hn_ref, _pack_halves(hn))
    e = lg_ref.shape[0]
    hn_hi = hn.astype(BF16)
    hn_lo = (hn - hn_hi.astype(F32)).astype(BF16)
    a = lax.dot_general(wr_ref[...], hn_hi, NT_DIMS, preferred_element_type=F32)
    b = lax.dot_general(wr_ref[0:e, :], hn_lo, NT_DIMS, preferred_element_type=F32)
    lg_ref[...] = a[0:e, :] + a[e:2 * e, :] + b


def _out_proj(mix_lru, att, x, g_att, w_o_bf, g_ffn, w_router_t, tm):
    t, d = x.shape
    c = mix_lru.shape[1]
    e = w_router_t.shape[0] // 2
    ns = d // 2 // LANES
    tm = min(tm, t)
    assert t % tm == 0
    row = lambda w: pl.BlockSpec((tm, w), lambda i: (i, 0))
    full = lambda a: pl.BlockSpec(a.shape, lambda i: (0, 0))
    return pl.pallas_call(
        _out_proj_kernel,
        grid=(t // tm,),
        in_specs=[row(c), row(c), row(d), full(g_att), full(w_o_bf), full(g_ffn),
                  full(w_router_t)],
        out_specs=[row(d), pl.BlockSpec((tm * ns, LANES), lambda i: (i, 0)),
                   pl.BlockSpec((e, tm), lambda i: (0, i))],
        out_shape=[SDS((t, d), F32), SDS((t * ns, LANES), U32), SDS((e, t), F32)],
        compiler_params=_params(1),
        name="out_proj",
    )(mix_lru, att, x, g_att, w_o_bf, g_ffn, w_router_t)


def _take_top(cur, idx, sentinel, n):
    picked = jnp.zeros(cur.shape, jnp.bool_)
    for _ in range(n):
        m = jnp.max(cur, axis=0, keepdims=True)
        first = jnp.min(jnp.where(cur == m, idx, sentinel), axis=0, keepdims=True)
        pick = idx == first
        picked = jnp.logical_or(picked, pick)
        cur = jnp.where(pick, -jnp.inf, cur)
    return picked


def _router_kernel(lg_ref, eb_ref, cin_ref, ek_ref, rk_ref, wk_ref, cnt_ref, run_ref):
    e, tm = lg_ref.shape
    per = e // N_GROUPS

    @pl.when(pl.program_id(0) == 0)
    def _():
        run_ref[...] = cin_ref[...]

    scores = jax.nn.sigmoid(lg_ref[...])
    biased = scores + eb_ref[...]
    sub = lax.broadcasted_iota(I32, (per, tm), 0)
    group_scores = []
    for g in range(N_GROUPS):
        bg = biased[g * per:(g + 1) * per, :]
        m1 = jnp.max(bg, axis=0, keepdims=True)
        first = jnp.min(jnp.where(bg == m1, sub, per), axis=0, keepdims=True)
        m2 = jnp.max(jnp.where(sub == first, -jnp.inf, bg), axis=0, keepdims=True)
        group_scores.append(m1 + m2)
    gs = jnp.concatenate(group_scores, axis=0)
    gidx = lax.broadcasted_iota(I32, gs.shape, 0)
    gsel = _take_top(gs, gidx, N_GROUPS, TOPK_GROUPS)
    emask = jnp.concatenate(
        [jnp.broadcast_to(gsel[g:g + 1, :], (per, tm)) for g in range(N_GROUPS)],
        axis=0)
    eidx = lax.broadcasted_iota(I32, (e, tm), 0)
    sel = _take_top(jnp.where(emask, biased, -jnp.inf), eidx, e, TOP_K)
    chosen = jnp.where(sel, scores, 0.0)
    gate = chosen / jnp.sum(chosen, axis=0, keepdims=True) * ROUTED_SCALE

    picked = jnp.where(sel, 1.0, 0.0)
    er = lax.broadcasted_iota(I32, (e, e), 0)
    ec = lax.broadcasted_iota(I32, (e, e), 1)
    slot = _dot(jnp.where(ec < er, 1.0, 0.0), picked)
    tr = lax.broadcasted_iota(I32, (tm, tm), 0)
    tc = lax.broadcasted_iota(I32, (tm, tm), 1)
    rank = _dot(picked, jnp.where(tr < tc, 1.0, 0.0)) + run_ref[...]
    run_ref[...] = run_ref[...] + jnp.sum(picked, axis=1, keepdims=True)
    cnt_ref[...] = run_ref[...]

    eidx_f = eidx.astype(F32)
    col = lambda m, v: jnp.sum(jnp.where(m, v, 0.0), axis=0, keepdims=True)
    eks, rks, wks = [], [], []
    for k in range(TOP_K):
        m = jnp.logical_and(sel, slot == k)
        eks.append(col(m, eidx_f))
        rks.append(col(m, rank))
        wks.append(col(m, gate))
    ek_ref[...] = jnp.concatenate(eks, axis=0).astype(I32)
    rk_ref[...] = jnp.concatenate(rks, axis=0).astype(I32)
    wk_ref[...] = jnp.concatenate(wks, axis=0)


def _router(logits_t, e_bias, counts_in, tm):
    e, t = logits_t.shape
    tm = min(tm, t)
    assert t % tm == 0
    blk = pl.BlockSpec((e, tm), lambda i: (0, i))
    vec = pl.BlockSpec((e, 1), lambda i: (0, 0))
    slot_blk = pl.BlockSpec((TOP_K, tm), lambda i: (0, i))
    return pl.pallas_call(
        _router_kernel,
        grid=(t // tm,),
        in_specs=[blk, vec, vec],
        out_specs=[slot_blk, slot_blk, slot_blk, vec],
        out_shape=[SDS((TOP_K, t), I32), SDS((TOP_K, t), I32), SDS((TOP_K, t), F32),
                   SDS((e, 1), F32)],
        scratch_shapes=[pltpu.VMEM((e, 1), F32)],
        compiler_params=_params(1),
        name="router",
    )(logits_t, e_bias.reshape(e, 1), counts_in)


def _positions_kernel(off_ref, ek_ref, rk_ref, pos_ref, *, ns):
    ek = ek_ref[...]
    pos = rk_ref[...]
    for e in range(off_ref.shape[0]):
        pos = pos + jnp.where(ek == e, off_ref[e], 0)
    pos_ref[...] = pos * ns


def _positions(offsets, ek, rk, ns, tm):
    k, t = ek.shape
    tm = min(tm, t)
    assert t % tm == 0
    blk = pl.BlockSpec((k, tm), lambda i: (0, i))
    return pl.pallas_call(
        functools.partial(_positions_kernel, ns=ns),
        grid=(t // tm,),
        in_specs=[pl.BlockSpec(memory_space=pltpu.SMEM), blk, blk],
        out_specs=blk,
        out_shape=SDS((k, t), I32),
        compiler_params=_params(1),
        name="positions",
    )(offsets, ek, rk)


def _work_list(counts, n_tiles, tm):
    e = counts.shape[0]
    ends = jnp.cumsum(counts)
    offsets = ends - counts
    n_work = n_tiles + e
    first_tile = offsets // tm
    n_e = jnp.where(counts > 0, (ends - 1) // tm - first_tile + 1, 0)
    w_end = jnp.cumsum(n_e)
    w_start = w_end - n_e
    total = w_end[-1]
    w = jnp.arange(n_work, dtype=I32)
    valid = w < total
    wc = jnp.minimum(w, total - 1)
    ex = jnp.minimum(jnp.sum((w_end[None, :] <= wc[:, None]).astype(I32), axis=1), e - 1)
    tile = first_tile[ex] + (wc - w_start[ex])
    lo = jnp.where(valid, jnp.maximum(offsets[ex], tile * tm) - tile * tm, 0)
    hi = jnp.where(valid, jnp.minimum(ends[ex], (tile + 1) * tm) - tile * tm, 0)
    prev_tile = jnp.concatenate([jnp.full((1,), -1, I32), tile[:-1]])
    next_tile = jnp.concatenate([tile[1:], jnp.full((1,), -1, I32)])
    prev_ex = jnp.concatenate([jnp.full((1,), -1, I32), ex[:-1]])
    first = valid & (tile != prev_tile)
    last = valid & ((tile != next_tile) | (w == total - 1))
    new_ex = valid & (ex != prev_ex)
    flags = first.astype(I32) + 2 * last.astype(I32) + 4 * new_ex.astype(I32)
    return (tile.astype(I32), ex, lo.astype(I32), hi.astype(I32), flags), offsets


def _dispatch_kernel(pos_ref, xa_ref, xb_ref, xs_ref, stage_ref, sem_ref, *pad, k, ns):
    tt = xa_ref.shape[0] // ns
    tb = xb_ref.shape[0] // ns
    s = pl.program_id(0)
    n = pl.num_programs(0)
    slot = lax.rem(s, 2)

    if pad:
        zero_ref, zsem_ref = pad
        n_pad = zero_ref.shape[0]
        fill = pltpu.make_async_copy(
            zero_ref, xs_ref.at[pl.ds(xs_ref.shape[0] - n_pad, n_pad)], zsem_ref.at[0])

        @pl.when(s == 0)
        def _():
            zero_ref[...] = jnp.zeros_like(zero_ref)
            fill.start()

        @pl.when(s == n - 1)
        def _():
            fill.wait()

    def wait_slot(sl, tokens):
        rows = tokens * ns
        for _ in range(k):
            pltpu.make_async_copy(stage_ref.at[sl, pl.ds(0, rows)], xs_ref.at[pl.ds(0, rows)],
                                  sem_ref.at[sl]).wait()

    def issue(tokens):
        def body(r, carry):
            src = stage_ref.at[slot, pl.ds(pl.multiple_of(r * ns, ns), ns)]
            for kk in range(k):
                dst = pl.multiple_of(pos_ref[0, 0, r * k + kk], ns)
                pltpu.make_async_copy(src, xs_ref.at[pl.ds(dst, ns)], sem_ref.at[slot]).start()
            return carry
        lax.fori_loop(0, tokens, body, 0)

    @pl.when(s >= 2)
    def _():
        wait_slot(slot, tt)

    @pl.when(s < n - 1)
    def _():
        stage_ref[slot] = xa_ref[...]
        issue(tt)

    @pl.when(s == n - 1)
    def _():
        stage_ref[slot, 0:tb * ns, :] = xb_ref[...]
        issue(tb)
        wait_slot(slot, tb)
        wait_slot(1 - slot, tt)


def _dispatch(xa, xb, pos_tk, n_slots, ns, tt):
    ta = xa.shape[0] // ns
    tb = xb.shape[0] // ns
    k = pos_tk.shape[1]
    tt = min(tt, ta)
    assert ta % tt == 0 and tb <= tt
    n_a = ta // tt
    pos_b = jnp.zeros((tt * k,), I32).at[:tb * k].set(pos_tk[ta:].reshape(-1))
    pos = jnp.concatenate([pos_tk[:ta].reshape(-1), pos_b]).reshape(n_a + 1, 1, tt * k)
    n_pad = n_slots - (ta + tb) * k
    scratch = [pltpu.VMEM((2, tt * ns, LANES), U32), pltpu.SemaphoreType.DMA((2,))]
    if n_pad:
        scratch += [pltpu.VMEM((n_pad * ns, LANES), U32), pltpu.SemaphoreType.DMA((1,))]
    return pl.pallas_call(
        functools.partial(_dispatch_kernel, k=k, ns=ns),
        grid=(n_a + 1,),
        in_specs=[pl.BlockSpec((1, 1, tt * k), lambda s: (s, 0, 0), memory_space=pltpu.SMEM),
                  pl.BlockSpec((tt * ns, LANES), lambda s: (jnp.minimum(s, n_a - 1), 0)),
                  pl.BlockSpec((tb * ns, LANES), lambda s: (0, 0))],
        out_specs=pl.BlockSpec(memory_space=pl.ANY),
        out_shape=SDS((n_slots * ns, LANES), U32),
        scratch_shapes=scratch,
        compiler_params=_params(1),
        name="dispatch",
    )(pos, xa, xb)


def _moe_kernel(tile_ref, ex_ref, lo_ref, hi_ref, flag_ref,
                xs_ref, wg_ref, wu_ref, wd_ref, ys_ref,
                wgb_ref, wub_ref, wdb_ref, acc_ref):
    del tile_ref, ex_ref
    w = pl.program_id(0)
    lo = lo_ref[w]
    hi = hi_ref[w]
    flags = flag_ref[w]

    @pl.when((flags & 4) != 0)
    def _():
        wgb_ref[...] = wg_ref[...].astype(BF16)
        wub_ref[...] = wu_ref[...].astype(BF16)
        wdb_ref[...] = wd_ref[...].astype(BF16)

    @pl.when(hi > lo)
    def _():
        x_lo, x_hi = _unpack_halves(_load_token_rows(xs_ref, acc_ref.shape[0]))
        x_lo = x_lo.astype(BF16)
        x_hi = x_hi.astype(BF16)
        half = x_lo.shape[1]
        hg = _dot(x_lo, wgb_ref[0:half, :]) + _dot(x_hi, wgb_ref[half:2 * half, :])
        hu = _dot(x_lo, wub_ref[0:half, :]) + _dot(x_hi, wub_ref[half:2 * half, :])
        rowi = lax.broadcasted_iota(I32, hg.shape, 0)
        mine = jnp.logical_and(rowi >= lo, rowi < hi)
        act = jnp.where(mine, _silu(hg) * hu, 0.0)
        y = _dot(act.astype(BF16), wdb_ref[...])

        @pl.when((flags & 1) != 0)
        def _():
            acc_ref[...] = y

        @pl.when((flags & 1) == 0)
        def _():
            acc_ref[...] += y

    @pl.when((flags & 2) != 0)
    def _():
        _store_token_rows(ys_ref, _pack_halves(acc_ref[...]))


def _moe(xs, work, w_eg, w_eu, w_ed, tm):
    e, d, f = w_eg.shape
    ns = d // 2 // LANES
    n_work = work[0].shape[0]
    row_spec = pl.BlockSpec((tm * ns, LANES), lambda w, tile, ex, lo, hi, fl: (tile[w], 0))
    up_spec = pl.BlockSpec((None, d, f), lambda w, tile, ex, lo, hi, fl: (ex[w], 0, 0))
    dn_spec = pl.BlockSpec((None, f, d), lambda w, tile, ex, lo, hi, fl: (ex[w], 0, 0))
    grid_spec = pltpu.PrefetchScalarGridSpec(
        num_scalar_prefetch=5,
        grid=(n_work,),
        in_specs=[row_spec, up_spec, up_spec, dn_spec],
        out_specs=row_spec,
        scratch_shapes=[pltpu.VMEM((d, f), BF16), pltpu.VMEM((d, f), BF16),
                        pltpu.VMEM((f, d), BF16), pltpu.VMEM((tm, d), F32)],
    )
    return pl.pallas_call(
        _moe_kernel,
        grid_spec=grid_spec,
        out_shape=SDS(xs.shape, U32),
        compiler_params=_params(1),
        name="moe",
    )(*work, xs, w_eg, w_eu, w_ed)


def _final_kernel(pos_ref, posn_ref, w_ref, h_ref, hn_ref, p_ref, ys_ref,
                  wsg_ref, wsu_ref, wsd_ref, gp_ref, wpg_ref, wpp_ref, y_ref,
                  gbuf_ref, sem_ref, *, tt, k):
    s = pl.program_id(0)
    n = pl.num_programs(0)
    slot = lax.rem(s, 2)

    ns = hn_ref.shape[0] // tt

    def issue(idx_ref, sl):
        def body(r, carry):
            row0 = pl.multiple_of(r * ns, ns)
            for kk in range(k):
                src = pl.multiple_of(idx_ref[0, 0, r * k + kk], ns)
                pltpu.make_async_copy(ys_ref.at[pl.ds(src, ns)],
                                      gbuf_ref.at[sl, kk, pl.ds(row0, ns)],
                                      sem_ref.at[sl]).start()
            return carry
        lax.fori_loop(0, tt, body, 0)

    @pl.when(s == 0)
    def _():
        issue(pos_ref, 0)

    @pl.when(s + 1 < n)
    def _():
        issue(posn_ref, 1 - slot)

    for kk in range(k):
        pltpu.make_async_copy(ys_ref.at[pl.ds(0, tt * ns)], gbuf_ref.at[slot, kk],
                              sem_ref.at[slot]).wait()

    wts = w_ref[...]
    r_lo = None
    for kk in range(k):
        lo, hi = _unpack_halves(_load_token_rows(gbuf_ref.at[slot, kk], tt))
        wk = wts[:, kk:kk + 1]
        r_lo = wk * lo if r_lo is None else r_lo + wk * lo
        r_hi = wk * hi if kk == 0 else r_hi + wk * hi
    routed = jnp.concatenate([r_lo, r_hi], axis=-1)

    n_lo, n_hi = _unpack_halves(_load_token_rows(hn_ref, tt))
    n_lo = n_lo.astype(BF16)
    n_hi = n_hi.astype(BF16)
    half = n_lo.shape[1]
    sg = _dot(n_lo, wsg_ref[0:half, :]) + _dot(n_hi, wsg_ref[half:2 * half, :])
    su = _dot(n_lo, wsu_ref[0:half, :]) + _dot(n_hi, wsu_ref[half:2 * half, :])
    shared = _dot((_silu(sg) * su).astype(BF16), wsd_ref[...])
    h2 = h_ref[...] + routed + shared
    gate = jax.nn.sigmoid(_dot(_rms(h2, gp_ref[...]).astype(BF16), wpg_ref[...]))
    y_ref[...] = h2 + gate * _dot(p_ref[...].astype(BF16), wpp_ref[...])


def _final(h, hn_packed, p, ys, pos_tk, w_tk, w_sg, w_su, w_sd, g_ple, w_pg, w_pp, tt):
    t, d = h.shape
    k = pos_tk.shape[1]
    ns = d // 2 // LANES
    tt = min(tt, t)
    assert t % tt == 0
    n = t // tt
    pos = pos_tk.reshape(n, 1, tt * k)
    row = lambda w: pl.BlockSpec((tt, w), lambda s: (s, 0))
    const = lambda a: pl.BlockSpec(a.shape, lambda s: (0, 0), pipeline_mode=pl.Buffered(1))
    smem = lambda fn: pl.BlockSpec((1, 1, tt * k), fn, memory_space=pltpu.SMEM)
    return pl.pallas_call(
        functools.partial(_final_kernel, tt=tt, k=k),
        grid=(n,),
        in_specs=[smem(lambda s: (s, 0, 0)),
                  smem(lambda s: (jnp.minimum(s + 1, n - 1), 0, 0)),
                  row(k), row(d), pl.BlockSpec((tt * ns, LANES), lambda s: (s, 0)),
                  row(p.shape[1]),
                  pl.BlockSpec(memory_space=pl.ANY),
                  const(w_sg), const(w_su), const(w_sd), const(g_ple), const(w_pg),
                  const(w_pp)],
        out_specs=row(d),
        out_shape=SDS((t, d), F32),
        scratch_shapes=[pltpu.VMEM((2, k, tt * ns, LANES), U32),
                        pltpu.SemaphoreType.DMA((2,))],
        compiler_params=_params(1),
        name="final",
    )(pos, pos, w_tk, h, hn_packed, p, ys, w_sg, w_su, w_sd, g_ple, w_pg, w_pp)


MOE_TILE = 512


def kernel(x_prompt, x_sample, p_prompt, p_sample, cache_k, cache_v, state_conv, state_h, page_table, g_mix, w_in, conv_w, conv_b, w_a, b_a, w_i, b_i, lru_lambda, q_gain, k_gain, sb_bias, g_lru_out, g_att_out, w_o, g_ffn, w_router, e_bias, w_eg, w_eu, w_ed, w_sg, w_su, w_sd, g_ple, w_ple_gate, w_ple_proj):
    depth = g_mix.shape[0]
    assert depth == 1
    batch, seq, d = x_prompt.shape
    dec_batch, dec_seq, _ = x_sample.shape
    assert dec_seq == 1
    n_heads, dh = cache_k.shape[3], cache_k.shape[4]
    c = n_heads * dh
    n_taps = conv_w.shape[1]
    assert w_in.shape[2] == 5 * c and seq >= n_taps - 1
    t_p = batch * seq

    row = lambda a: a.reshape(1, -1)
    wai = jnp.concatenate([w_a[0], w_i[0]], axis=-1).astype(BF16)
    lru_w = (conv_w[0], row(conv_b[0]), wai, row(b_a[0]), row(b_i[0]),
             row(lru_lambda[0]), row(g_lru_out[0]))
    in_proj = functools.partial(_in_proj, g_mix=row(g_mix[0]), w_in_bf=w_in[0].astype(BF16),
                                q_gain=row(q_gain[0]), k_gain=row(k_gain[0]),
                                n_heads=n_heads)
    wr_t = w_router[0].T
    wr_hi = wr_t.astype(BF16)
    wr_lo = (wr_t - wr_hi.astype(F32)).astype(BF16)
    out_proj = functools.partial(_out_proj, g_att=row(g_att_out[0]), w_o_bf=w_o[0].astype(BF16),
                                 g_ffn=row(g_ffn[0]),
                                 w_router_t=jnp.concatenate([wr_hi, wr_lo], axis=0))
    final = functools.partial(
        _final, w_sg=w_sg[0].astype(BF16), w_su=w_su[0].astype(BF16),
        w_sd=w_sd[0].astype(BF16), g_ple=row(g_ple[0]), w_pg=w_ple_gate[0].astype(BF16),
        w_pp=w_ple_proj[0].astype(BF16))

    xp = x_prompt.reshape(t_p, d)
    xr, gy, q, k, v, kb, vb = in_proj(xp, tm=512)
    mix_lru, h_last = _lru_prompt(
        xr.reshape(batch, seq, c), gy.reshape(batch, seq, c),
        jnp.zeros((batch, 8, c), F32), jnp.zeros((batch, 1, c), F32), *lru_w, tl=256)
    att = _attn_prompt(q, kb, vb, sb_bias[0], batch, n_heads, tq=512, tk=256, hp=4)
    h_p, hn_p, logits_p = out_proj(mix_lru.reshape(t_p, c), att, xp, tm=256)

    xs_ = x_sample.reshape(dec_batch, d)
    xr_s, gy_s, q_s, k_s, v_s, _, _ = in_proj(xs_, tm=dec_batch)
    mix_lru_s, h_s = _lru_step(xr_s, gy_s, jnp.swapaxes(state_conv[0], 0, 1),
                               state_h[0], *lru_w)
    att_s = _attn_sample(q_s.reshape(dec_batch, n_heads, dh), cache_k[0], cache_v[0],
                         page_table, sb_bias[0], n_pages_step=8)
    h_s2, hn_s, logits_s = out_proj(mix_lru_s, att_s.reshape(dec_batch, c), xs_, tm=dec_batch)

    n_experts = w_router.shape[2]
    ek_p, rk_p, wk_p, counts_p = _router(logits_p, e_bias[0],
                                         jnp.zeros((n_experts, 1), F32), tm=256)
    ek_s, rk_s, wk_s, counts = _router(logits_s, e_bias[0], counts_p, tm=256)
    n_tiles = pl.cdiv((t_p + dec_batch) * TOP_K, MOE_TILE)
    work, offsets = _work_list(counts[:, 0].astype(I32), n_tiles, MOE_TILE)
    ns = d // 2 // LANES
    pos_p = _positions(offsets, ek_p, rk_p, ns, tm=512).T
    pos_s = _positions(offsets, ek_s, rk_s, ns, tm=512).T
    rows = _dispatch(hn_p, hn_s, jnp.concatenate([pos_p, pos_s]), n_tiles * MOE_TILE, ns,
                     tt=128)
    ys = _moe(rows, work, w_eg[0], w_eu[0], w_ed[0], MOE_TILE)
    y_prompt = final(h_p, hn_p, p_prompt[0].reshape(t_p, -1), ys, pos_p, wk_p.T, tt=256)
    y_sample = final(h_s2, hn_s, p_sample[0].reshape(dec_batch, -1), ys, pos_s, wk_s.T,
                     tt=256)

    k_prompt = k.reshape(1, batch, seq, n_heads, dh)
    v_prompt = v.reshape(1, batch, seq, n_heads, dh)
    conv_prompt = xr.reshape(batch, seq, c)[:, seq - (n_taps - 1):][None]
    h_prompt = h_last.reshape(1, batch, c)
    k_sample = k_s.reshape(1, dec_batch, 1, n_heads, dh)
    v_sample = v_s.reshape(1, dec_batch, 1, n_heads, dh)
    conv_sample = jnp.concatenate([state_conv[0][:, 1:], xr_s[:, None]], axis=1)[None]
    h_sample = h_s.reshape(1, dec_batch, c)

    return (y_prompt.reshape(batch, seq, d), y_sample.reshape(dec_batch, 1, d),
            k_prompt, v_prompt, conv_prompt, h_prompt,
            k_sample, v_sample, conv_sample, h_sample)
```

```python
import functools

import jax
import jax.numpy as jnp
from jax import lax
from jax.experimental import pallas as pl
from jax.experimental.pallas import tpu as pltpu

F32 = jnp.float32
BF16 = jnp.bfloat16
U32 = jnp.uint32
I32 = jnp.int32
SDS = jax.ShapeDtypeStruct

EPS = 1e-6
LRU_C = 8.0
N_GROUPS = 8
TOPK_GROUPS = 4
TOP_K = 8
ROUTED_SCALE = 2.5

VMEM_LIMIT_BYTES = 56 * 1024 * 1024

NT_DIMS = (((1,), (1,)), ((), ()))


def _params(n_grid_axes):
    return pltpu.CompilerParams(
        dimension_semantics=("arbitrary",) * n_grid_axes,
        vmem_limit_bytes=VMEM_LIMIT_BYTES,
    )


def _rms(x, g):
    return x * lax.rsqrt(jnp.mean(x * x, axis=-1, keepdims=True) + EPS) * g


def _softplus(x):
    return jnp.maximum(x, 0.0) + jnp.log(1.0 + jnp.exp(-jnp.abs(x)))


def _silu(x):
    return x * jax.nn.sigmoid(x)


def _dot(a, b):
    return jnp.dot(a, b, preferred_element_type=F32)


def _pack_halves(x):
    half = x.shape[1] // 2
    return pltpu.pack_elementwise([x[:, :half], x[:, half:]], packed_dtype=BF16)


LANES = 128


def _store_token_rows(ref, packed):
    m, w = packed.shape
    ns = w // LANES
    for s in range(ns):
        ref[pl.ds(s, m, stride=ns), :] = packed[:, s * LANES:(s + 1) * LANES]


def _load_token_rows(ref, m):
    ns = ref.shape[0] // m
    return jnp.concatenate([ref[pl.ds(s, m, stride=ns), :] for s in range(ns)], axis=-1)


def _unpack_halves(w):
    unpack = functools.partial(pltpu.unpack_elementwise, w, packed_dtype=BF16,
                               unpacked_dtype=F32)
    return unpack(index=0), unpack(index=1)


def _head_rms(z, gain, n_heads):
    dh = z.shape[1] // n_heads
    return jnp.concatenate(
        [_rms(z[:, h * dh:(h + 1) * dh], gain) for h in range(n_heads)], axis=-1)


def _in_proj_kernel(x_ref, g_ref, w_ref, qg_ref, kg_ref,
                    xr_ref, gy_ref, q_ref, k_ref, v_ref, kb_ref, vb_ref,
                    xn_ref, *, n_heads):
    j = pl.program_id(1)

    @pl.when(j == 0)
    def _():
        xn_ref[...] = _rms(x_ref[...], g_ref[...]).astype(BF16)

    def z():
        return _dot(xn_ref[...], w_ref[...])

    @pl.when(j == 0)
    def _():
        xr_ref[...] = z()

    @pl.when(j == 1)
    def _():
        gy_ref[...] = jax.nn.gelu(z())

    @pl.when(j == 2)
    def _():
        dh = q_ref.shape[1] // n_heads
        q_ref[...] = (_head_rms(z(), qg_ref[...], n_heads) * dh ** -0.5).astype(BF16)

    @pl.when(j == 3)
    def _():
        k = _head_rms(z(), kg_ref[...], n_heads)
        k_ref[...] = k
        kb_ref[...] = k.astype(BF16)

    @pl.when(j == 4)
    def _():
        v = z()
        v_ref[...] = v
        vb_ref[...] = v.astype(BF16)


def _in_proj(x, g_mix, w_in_bf, q_gain, k_gain, n_heads, tm):
    t, d = x.shape
    c = w_in_bf.shape[1] // 5
    dh = c // n_heads
    tm = min(tm, t)
    assert t % tm == 0
    row = lambda i, j: (i, 0)
    const = lambda i, j: (0, 0)
    out_blk = pl.BlockSpec((tm, c), row)
    return pl.pallas_call(
        functools.partial(_in_proj_kernel, n_heads=n_heads),
        grid=(t // tm, 5),
        in_specs=[
            pl.BlockSpec((tm, d), row),
            pl.BlockSpec((1, d), const),
            pl.BlockSpec((d, c), lambda i, j: (0, j)),
            pl.BlockSpec((1, dh), const),
            pl.BlockSpec((1, dh), const),
        ],
        out_specs=[out_blk] * 7,
        out_shape=[SDS((t, c), F32), SDS((t, c), F32), SDS((t, c), BF16),
                   SDS((t, c), F32), SDS((t, c), F32), SDS((t, c), BF16),
                   SDS((t, c), BF16)],
        scratch_shapes=[pltpu.VMEM((tm, d), BF16)],
        compiler_params=_params(2),
        name="in_proj",
    )(x, g_mix, w_in_bf, q_gain, k_gain)


def _lru_gates(xc, wai_ref, ba, bi, lam):
    n_blocks, blk, _ = wai_ref.shape
    xcb = xc.astype(BF16)
    ga, gi = [], []
    for n in range(n_blocks):
        g = _dot(xcb[:, n * blk:(n + 1) * blk], wai_ref[n])
        ga.append(g[:, :blk])
        gi.append(g[:, blk:])
    r = jax.nn.sigmoid(jnp.concatenate(ga, axis=-1) + ba)
    ig = jax.nn.sigmoid(jnp.concatenate(gi, axis=-1) + bi)
    log_a = -LRU_C * r * _softplus(-lam)
    a = jnp.exp(log_a)
    th = jnp.tanh(log_a)
    u = jnp.sqrt(-2.0 * th / (1.0 - th)) * ig * xc
    return a, u


def _lru_prompt_kernel(xr_ref, gy_ref, st_ref, h0_ref, cw_ref, cb_ref, wai_ref,
                       ba_ref, bi_ref, lam_ref, gl_ref, mix_ref, ht_ref,
                       xext_ref, a_ref, u_ref, h_ref, *, tl):
    l = pl.program_id(1)

    @pl.when(l == 0)
    def _():
        xext_ref[0:8, :] = st_ref[0]
        h_ref[...] = h0_ref[0]

    xext_ref[8:8 + tl, :] = xr_ref[0]
    cw = cw_ref[...]
    n_taps = cw.shape[0]
    xc = cb_ref[...]
    for j in range(n_taps):
        off = 8 - (n_taps - 1) + j
        xc = xc + cw[j:j + 1, :] * xext_ref[off:off + tl, :]
    xext_ref[0:8, :] = xext_ref[tl:tl + 8, :]

    a, u = _lru_gates(xc, wai_ref, ba_ref[...], bi_ref[...], lam_ref[...])
    a_ref[...] = a
    u_ref[...] = u

    def step(t, h):
        h = a_ref[pl.ds(t, 1), :] * h + u_ref[pl.ds(t, 1), :]
        u_ref[pl.ds(t, 1), :] = h
        return h

    h = lax.fori_loop(0, tl, step, h_ref[...], unroll=8)
    h_ref[...] = h
    mix_ref[0] = _rms(u_ref[...] * gy_ref[0], gl_ref[...]).astype(BF16)

    @pl.when(l == pl.num_programs(1) - 1)
    def _():
        ht_ref[0] = h


def _lru_prompt(xr, gy, state8, h0, conv_w, conv_b, wai, b_a, b_i, lam, g_lru, tl):
    b, l, c = xr.shape
    tl = min(tl, l)
    assert l % tl == 0 and tl % 8 == 0
    tile = pl.BlockSpec((1, tl, c), lambda i, j: (i, j, 0))
    per_b = lambda rows: pl.BlockSpec((1, rows, c), lambda i, j: (i, 0, 0))
    vec = pl.BlockSpec((1, c), lambda i, j: (0, 0))
    return pl.pallas_call(
        functools.partial(_lru_prompt_kernel, tl=tl),
        grid=(b, l // tl),
        in_specs=[tile, tile, per_b(8), per_b(1),
                  pl.BlockSpec(conv_w.shape, lambda i, j: (0, 0)), vec,
                  pl.BlockSpec(wai.shape, lambda i, j: (0, 0, 0)),
                  vec, vec, vec, vec],
        out_specs=[tile, per_b(1)],
        out_shape=[SDS((b, l, c), BF16), SDS((b, 1, c), F32)],
        scratch_shapes=[pltpu.VMEM((tl + 8, c), F32), pltpu.VMEM((tl, c), F32),
                        pltpu.VMEM((tl, c), F32), pltpu.VMEM((1, c), F32)],
        compiler_params=_params(2),
        name="lru_prompt",
    )(xr, gy, state8, h0, conv_w, conv_b, wai, b_a, b_i, lam, g_lru)


def _lru_step_kernel(xr_ref, gy_ref, st_ref, h0_ref, cw_ref, cb_ref, wai_ref,
                     ba_ref, bi_ref, lam_ref, gl_ref, mix_ref, hn_ref):
    cw = cw_ref[...]
    n_taps = cw.shape[0]
    xc = cb_ref[...] + cw[n_taps - 1:n_taps, :] * xr_ref[...]
    for j in range(n_taps - 1):
        xc = xc + cw[j:j + 1, :] * st_ref[j]
    a, u = _lru_gates(xc, wai_ref, ba_ref[...], bi_ref[...], lam_ref[...])
    h = a * h0_ref[...] + u
    hn_ref[...] = h
    mix_ref[...] = _rms(h * gy_ref[...], gl_ref[...]).astype(BF16)


def _lru_step(xr, gy, state, h0, conv_w, conv_b, wai, b_a, b_i, lam, g_lru):
    b, c = xr.shape
    return pl.pallas_call(
        _lru_step_kernel,
        out_shape=[SDS((b, c), BF16), SDS((b, c), F32)],
        compiler_params=pltpu.CompilerParams(vmem_limit_bytes=VMEM_LIMIT_BYTES),
        name="lru_step",
    )(xr, gy, state, h0, conv_w, conv_b, wai, b_a, b_i, lam, g_lru)


def _attn_prompt_kernel(bias_ref, q_ref, k_ref, v_ref, tri_ref, o_ref, *, tq, tk, hp):
    hg = pl.program_id(1)
    i = pl.program_id(2)
    tri = tri_ref[...]
    dh = q_ref.shape[1] // hp
    r = tq // tk
    qs = [q_ref[:, a * dh:(a + 1) * dh] for a in range(hp)]
    biases = [bias_ref[hg * hp + a] for a in range(hp)]
    row = lax.broadcasted_iota(I32, (tq, tk), 0)
    col = lax.broadcasted_iota(I32, (tq, tk), 1)

    def tile(j, carry, diag):
        start = pl.multiple_of(j * tk, tk)
        new = []
        for a in range(hp):
            c, acc = carry[a]
            kj = k_ref[pl.ds(start, tk), a * dh:(a + 1) * dh]
            vj = v_ref[pl.ds(start, tk), a * dh:(a + 1) * dh]
            z = lax.dot_general(qs[a], kj, NT_DIMS, preferred_element_type=F32) + biases[a]
            sp = _softplus(z)
            if diag is not None:
                causal = col + diag * tk < row
                sp = jnp.where(causal, sp, 0.0)
            drop = _dot(sp.astype(BF16), tri)
            w = jnp.exp(z - c - drop)
            if diag is not None:
                w = jnp.where(causal, w, 0.0)
            acc = acc + _dot(w.astype(BF16), vj)
            c = c + drop[:, 0:1]
            new.append((c, acc))
        return tuple(new)

    carry = tuple((jnp.zeros((tq, 1), F32), jnp.zeros((tq, dh), F32)) for _ in range(hp))
    for dj in reversed(range(r)):
        carry = tile(i * r + dj, carry, dj)
    carry = lax.fori_loop(0, i * r, lambda jj, cr: tile(i * r - 1 - jj, cr, None), carry)
    o_ref[...] = jnp.concatenate([carry[a][1] for a in range(hp)], axis=-1)


def _attn_prompt(q, k, v, sb_bias, batch, n_heads, tq, tk, hp):
    t, c = q.shape
    l = t // batch
    dh = c // n_heads
    tq = min(tq, l)
    tk = min(tk, tq)
    hp = min(hp, n_heads)
    assert l % tq == 0 and tq % tk == 0 and n_heads % hp == 0
    nq = l // tq
    tri = (jnp.arange(tk)[:, None] >= jnp.arange(tk)[None, :]).astype(BF16)
    kv_spec = pl.BlockSpec((l, hp * dh), lambda b, h, i: (b, h))
    q_spec = pl.BlockSpec((tq, hp * dh), lambda b, h, i: (b * nq + i, h))
    return pl.pallas_call(
        functools.partial(_attn_prompt_kernel, tq=tq, tk=tk, hp=hp),
        grid=(batch, n_heads // hp, nq),
        in_specs=[pl.BlockSpec(memory_space=pltpu.SMEM), q_spec, kv_spec, kv_spec,
                  pl.BlockSpec((tk, tk), lambda b, h, i: (0, 0))],
        out_specs=q_spec,
        out_shape=SDS((t, c), F32),
        compiler_params=_params(3),
        name="attn_prompt",
    )(sb_bias, q, k, v, tri)


def _shift_lanes_left(x, s):
    n = x.shape[1]
    col = lax.broadcasted_iota(I32, x.shape, 1)
    return jnp.where(col < n - s, pltpu.roll(x, n - s, axis=1), 0.0)


def _shift_rows_down(x, s):
    rowi = lax.broadcasted_iota(I32, x.shape, 0)
    return jnp.where(rowi >= s, pltpu.roll(x, s, axis=0), 0.0)


def _attn_sample_kernel(pt_ref, bias_ref, q_ref, *refs, n_pages_step):
    del pt_ref
    g_ = n_pages_step
    k_refs = refs[:g_]
    v_refs = refs[g_:2 * g_]
    o_ref, c_ref, acc_ref = refs[2 * g_:]
    step = pl.program_id(1)

    @pl.when(step == 0)
    def _():
        c_ref[...] = jnp.zeros_like(c_ref)
        acc_ref[...] = jnp.zeros_like(acc_ref)

    q = q_ref[0]
    n_heads, dh = q.shape
    page = k_refs[0].shape[0]
    n_col = page * n_heads
    row = lax.broadcasted_iota(I32, (n_heads, n_col), 0)
    col = lax.broadcasted_iota(I32, (n_heads, n_col), 1)
    own = lax.rem(col, n_heads) == row
    pick = lambda t: jnp.sum(jnp.where(own, t, 0.0), axis=0, keepdims=True)
    bias_row = pick(jnp.broadcast_to(bias_ref[...], (n_heads, n_col)))
    prow = lax.broadcasted_iota(I32, (g_, n_col), 0)

    z = jnp.zeros((g_, n_col), F32)
    for g in range(g_):
        kp = k_refs[g][...].reshape(n_col, dh).astype(BF16)
        z8 = lax.dot_general(q, kp, NT_DIMS, preferred_element_type=F32)
        z = jnp.where(prow == g, jnp.broadcast_to(pick(z8), (g_, n_col)), z)
    z = z + bias_row
    sp = _softplus(z)
    log_keep = -sp
    log_after = _shift_lanes_left(log_keep, n_heads)
    total = log_keep
    s = n_heads
    while s < n_col:
        log_after = log_after + _shift_lanes_left(log_after, s)
        total = total + pltpu.roll(total, s, axis=1)
        s *= 2
    before = _shift_rows_down(total, 1)
    s = 1
    while s < g_:
        before = before + _shift_rows_down(before, s)
        s *= 2
    c_in = c_ref[...]
    w = jnp.exp(z - sp + log_after + before + c_in)
    c_ref[...] = c_in + jnp.sum(total, axis=0, keepdims=True)

    acc = acc_ref[...]
    for g in range(g_):
        w8 = jnp.where(own, jnp.broadcast_to(w[g:g + 1, :], (n_heads, n_col)), 0.0)
        vp = v_refs[g][...].reshape(n_col, dh).astype(BF16)
        acc = acc + _dot(w8.astype(BF16), vp)
    acc_ref[...] = acc

    @pl.when(step == pl.num_programs(1) - 1)
    def _():
        o_ref[0] = acc


def _attn_sample(q, cache_k, cache_v, page_table, sb_bias, n_pages_step):
    b, n_heads, dh = q.shape
    _, page, _, _ = cache_k.shape
    n_pages = page_table.shape[1]
    g_ = min(n_pages_step, n_pages)
    assert n_pages % g_ == 0 and page & (page - 1) == 0

    def page_spec(g):
        return pl.BlockSpec(
            (None, page, n_heads, dh),
            lambda i, j, pt: (pt[i, n_pages - 1 - (j * g_ + g)], 0, 0, 0))

    q_spec = pl.BlockSpec((1, n_heads, dh), lambda i, j, pt: (i, 0, 0))
    grid_spec = pltpu.PrefetchScalarGridSpec(
        num_scalar_prefetch=1,
        grid=(b, n_pages // g_),
        in_specs=[pl.BlockSpec((n_heads, 1), lambda i, j, pt: (0, 0)), q_spec]
        + [page_spec(g) for g in range(g_)] * 2,
        out_specs=q_spec,
        scratch_shapes=[pltpu.VMEM((1, page * n_heads), F32),
                        pltpu.VMEM((n_heads, dh), F32)],
    )
    return pl.pallas_call(
        functools.partial(_attn_sample_kernel, n_pages_step=g_),
        grid_spec=grid_spec,
        out_shape=SDS((b, n_heads, dh), F32),
        compiler_params=_params(2),
        name="attn_sample",
    )(page_table, sb_bias.reshape(n_heads, 1), q,
      *([cache_k] * g_), *([cache_v] * g_))


def _out_proj_kernel(ml_ref, att_ref, x_ref, ga_ref, wo_ref, gf_ref, wr_ref,
                     h_ref, hn_ref, lg_ref):
    c = ml_ref.shape[1]
    attn = _rms(att_ref[...], ga_ref[...]).astype(BF16)
    mix = _dot(ml_ref[...], wo_ref[0:c, :]) + _dot(attn, wo_ref[c:2 * c, :])
    h = x_ref[...] + mix
    h_ref[...] = h
    hn = _rms(h, gf_ref[...])
    _store_token_rows(hn_ref, _pack_halves(hn))
    e = lg_ref.shape[0]
    hn_hi = hn.astype(BF16)
    hn_lo = (hn - hn_hi.astype(F32)).astype(BF16)
    a = lax.dot_general(wr_ref[...], hn_hi, NT_DIMS, preferred_element_type=F32)
    b = lax.dot_general(wr_ref[0:e, :], hn_lo, NT_DIMS, preferred_element_type=F32)
    lg_ref[...] = a[0:e, :] + a[e:2 * e, :] + b


def _out_proj(mix_lru, att, x, g_att, w_o_bf, g_ffn, w_router_t, tm):
    t, d = x.shape
    c = mix_lru.shape[1]
    e = w_router_t.shape[0] // 2
    ns = d // 2 // LANES
    tm = min(tm, t)
    assert t % tm == 0
    row = lambda w: pl.BlockSpec((tm, w), lambda i: (i, 0))
    full = lambda a: pl.BlockSpec(a.shape, lambda i: (0, 0))
    return pl.pallas_call(
        _out_proj_kernel,
        grid=(t // tm,),
        in_specs=[row(c), row(c), row(d), full(g_att), full(w_o_bf), full(g_ffn),
                  full(w_router_t)],
        out_specs=[row(d), pl.BlockSpec((tm * ns, LANES), lambda i: (i, 0)),
                   pl.BlockSpec((e, tm), lambda i: (0, i))],
        out_shape=[SDS((t, d), F32), SDS((t * ns, LANES), U32), SDS((e, t), F32)],
        compiler_params=_params(1),
        name="out_proj",
    )(mix_lru, att, x, g_att, w_o_bf, g_ffn, w_router_t)


def _take_top(cur, idx, sentinel, n):
    picked = jnp.zeros(cur.shape, jnp.bool_)
    for _ in range(n):
        m = jnp.max(cur, axis=0, keepdims=True)
        first = jnp.min(jnp.where(cur == m, idx, sentinel), axis=0, keepdims=True)
        pick = idx == first
        picked = jnp.logical_or(picked, pick)
        cur = jnp.where(pick, -jnp.inf, cur)
    return picked


def _router_kernel(lg_ref, eb_ref, cin_ref, ek_ref, rk_ref, wk_ref, cnt_ref, run_ref):
    e, tm = lg_ref.shape
    per = e // N_GROUPS

    @pl.when(pl.program_id(0) == 0)
    def _():
        run_ref[...] = cin_ref[...]

    scores = jax.nn.sigmoid(lg_ref[...])
    biased = scores + eb_ref[...]
    sub = lax.broadcasted_iota(I32, (per, tm), 0)
    group_scores = []
    for g in range(N_GROUPS):
        bg = biased[g * per:(g + 1) * per, :]
        m1 = jnp.max(bg, axis=0, keepdims=True)
        first = jnp.min(jnp.where(bg == m1, sub, per), axis=0, keepdims=True)
        m2 = jnp.max(jnp.where(sub == first, -jnp.inf, bg), axis=0, keepdims=True)
        group_scores.append(m1 + m2)
    gs = jnp.concatenate(group_scores, axis=0)
    gidx = lax.broadcasted_iota(I32, gs.shape, 0)
    gsel = _take_top(gs, gidx, N_GROUPS, TOPK_GROUPS)
    emask = jnp.concatenate(
        [jnp.broadcast_to(gsel[g:g + 1, :], (per, tm)) for g in range(N_GROUPS)],
        axis=0)
    eidx = lax.broadcasted_iota(I32, (e, tm), 0)
    sel = _take_top(jnp.where(emask, biased, -jnp.inf), eidx, e, TOP_K)
    chosen = jnp.where(sel, scores, 0.0)
    gate = chosen / jnp.sum(chosen, axis=0, keepdims=True) * ROUTED_SCALE

    picked = jnp.where(sel, 1.0, 0.0)
    er = lax.broadcasted_iota(I32, (e, e), 0)
    ec = lax.broadcasted_iota(I32, (e, e), 1)
    slot = _dot(jnp.where(ec < er, 1.0, 0.0), picked)
    tr = lax.broadcasted_iota(I32, (tm, tm), 0)
    tc = lax.broadcasted_iota(I32, (tm, tm), 1)
    rank = _dot(picked, jnp.where(tr < tc, 1.0, 0.0)) + run_ref[...]
    run_ref[...] = run_ref[...] + jnp.sum(picked, axis=1, keepdims=True)
    cnt_ref[...] = run_ref[...]

    eidx_f = eidx.astype(F32)
    col = lambda m, v: jnp.sum(jnp.where(m, v, 0.0), axis=0, keepdims=True)
    eks, rks, wks = [], [], []
    for k in range(TOP_K):
        m = jnp.logical_and(sel, slot == k)
        eks.append(col(m, eidx_f))
        rks.append(col(m, rank))
        wks.append(col(m, gate))
    ek_ref[...] = jnp.concatenate(eks, axis=0).astype(I32)
    rk_ref[...] = jnp.concatenate(rks, axis=0).astype(I32)
    wk_ref[...] = jnp.concatenate(wks, axis=0)


def _router(logits_t, e_bias, counts_in, tm):
    e, t = logits_t.shape
    tm = min(tm, t)
    assert t % tm == 0
    blk = pl.BlockSpec((e, tm), lambda i: (0, i))
    vec = pl.BlockSpec((e, 1), lambda i: (0, 0))
    slot_blk = pl.BlockSpec((TOP_K, tm), lambda i: (0, i))
    return pl.pallas_call(
        _router_kernel,
        grid=(t // tm,),
        in_specs=[blk, vec, vec],
        out_specs=[slot_blk, slot_blk, slot_blk, vec],
        out_shape=[SDS((TOP_K, t), I32), SDS((TOP_K, t), I32), SDS((TOP_K, t), F32),
                   SDS((e, 1), F32)],
        scratch_shapes=[pltpu.VMEM((e, 1), F32)],
        compiler_params=_params(1),
        name="router",
    )(logits_t, e_bias.reshape(e, 1), counts_in)


def _positions_kernel(off_ref, ek_ref, rk_ref, pos_ref, *, ns):
    ek = ek_ref[...]
    pos = rk_ref[...]
    for e in range(off_ref.shape[0]):
        pos = pos + jnp.where(ek == e, off_ref[e], 0)
    pos_ref[...] = pos * ns


def _positions(offsets, ek, rk, ns, tm):
    k, t = ek.shape
    tm = min(tm, t)
    assert t % tm == 0
    blk = pl.BlockSpec((k, tm), lambda i: (0, i))
    return pl.pallas_call(
        functools.partial(_positions_kernel, ns=ns),
        grid=(t // tm,),
        in_specs=[pl.BlockSpec(memory_space=pltpu.SMEM), blk, blk],
        out_specs=blk,
        out_shape=SDS((k, t), I32),
        compiler_params=_params(1),
        name="positions",
    )(offsets, ek, rk)


def _work_list(counts, n_tiles, tm):
    e = counts.shape[0]
    ends = jnp.cumsum(counts)
    offsets = ends - counts
    n_work = n_tiles + e
    first_tile = offsets // tm
    n_e = jnp.where(counts > 0, (ends - 1) // tm - first_tile + 1, 0)
    w_end = jnp.cumsum(n_e)
    w_start = w_end - n_e
    total = w_end[-1]
    w = jnp.arange(n_work, dtype=I32)
    valid = w < total
    wc = jnp.minimum(w, total - 1)
    ex = jnp.minimum(jnp.sum((w_end[None, :] <= wc[:, None]).astype(I32), axis=1), e - 1)
    onehot = ex[:, None] == jnp.arange(e, dtype=I32)[None, :]
    take = lambda v: jnp.sum(jnp.where(onehot, v[None, :], 0), axis=1)
    tile = take(first_tile) + (wc - take(w_start))
    lo = jnp.where(valid, jnp.maximum(take(offsets), tile * tm) - tile * tm, 0)
    hi = jnp.where(valid, jnp.minimum(take(ends), (tile + 1) * tm) - tile * tm, 0)
    prev_tile = jnp.concatenate([jnp.full((1,), -1, I32), tile[:-1]])
    next_tile = jnp.concatenate([tile[1:], jnp.full((1,), -1, I32)])
    prev_ex = jnp.concatenate([jnp.full((1,), -1, I32), ex[:-1]])
    first = valid & (tile != prev_tile)
    last = valid & ((tile != next_tile) | (w == total - 1))
    new_ex = valid & (ex != prev_ex)
    flags = first.astype(I32) + 2 * last.astype(I32) + 4 * new_ex.astype(I32)
    return (tile.astype(I32), ex, lo.astype(I32), hi.astype(I32), flags), offsets


def _dispatch_kernel(pos_ref, xa_ref, xb_ref, xs_ref, stage_ref, sem_ref, *pad, k, ns):
    tt = xa_ref.shape[0] // ns
    tb = xb_ref.shape[0] // ns
    s = pl.program_id(0)
    n = pl.num_programs(0)
    slot = lax.rem(s, 2)

    if pad:
        zero_ref, zsem_ref = pad
        n_pad = zero_ref.shape[0]
        fill = pltpu.make_async_copy(
            zero_ref, xs_ref.at[pl.ds(xs_ref.shape[0] - n_pad, n_pad)], zsem_ref.at[0])

        @pl.when(s == 0)
        def _():
            zero_ref[...] = jnp.zeros_like(zero_ref)
            fill.start()

        @pl.when(s == n - 1)
        def _():
            fill.wait()

    def wait_slot(sl, tokens):
        rows = tokens * ns
        for _ in range(k):
            pltpu.make_async_copy(stage_ref.at[sl, pl.ds(0, rows)], xs_ref.at[pl.ds(0, rows)],
                                  sem_ref.at[sl]).wait()

    def issue(tokens):
        def body(r, carry):
            src = stage_ref.at[slot, pl.ds(pl.multiple_of(r * ns, ns), ns)]
            for kk in range(k):
                dst = pl.multiple_of(pos_ref[0, 0, r * k + kk], ns)
                pltpu.make_async_copy(src, xs_ref.at[pl.ds(dst, ns)],
                                      sem_ref.at[slot]).start(priority=kk % 2)
            return carry
        lax.fori_loop(0, tokens, body, 0)

    @pl.when(s >= 2)
    def _():
        wait_slot(slot, tt)

    @pl.when(s < n - 1)
    def _():
        stage_ref[slot] = xa_ref[...]
        issue(tt)

    @pl.when(s == n - 1)
    def _():
        stage_ref[slot, 0:tb * ns, :] = xb_ref[...]
        issue(tb)
        wait_slot(slot, tb)
        wait_slot(1 - slot, tt)


def _dispatch(xa, xb, pos_tk, n_slots, ns, tt):
    ta = xa.shape[0] // ns
    tb = xb.shape[0] // ns
    k = pos_tk.shape[1]
    tt = min(tt, ta)
    assert ta % tt == 0 and tb <= tt
    n_a = ta // tt
    pos_b = jnp.zeros((tt * k,), I32).at[:tb * k].set(pos_tk[ta:].reshape(-1))
    pos = jnp.concatenate([pos_tk[:ta].reshape(-1), pos_b]).reshape(n_a + 1, 1, tt * k)
    n_pad = n_slots - (ta + tb) * k
    scratch = [pltpu.VMEM((2, tt * ns, LANES), U32), pltpu.SemaphoreType.DMA((2,))]
    if n_pad:
        scratch += [pltpu.VMEM((n_pad * ns, LANES), U32), pltpu.SemaphoreType.DMA((1,))]
    return pl.pallas_call(
        functools.partial(_dispatch_kernel, k=k, ns=ns),
        grid=(n_a + 1,),
        in_specs=[pl.BlockSpec((1, 1, tt * k), lambda s: (s, 0, 0), memory_space=pltpu.SMEM),
                  pl.BlockSpec((tt * ns, LANES), lambda s: (jnp.minimum(s, n_a - 1), 0)),
                  pl.BlockSpec((tb * ns, LANES), lambda s: (0, 0))],
        out_specs=pl.BlockSpec(memory_space=pl.ANY),
        out_shape=SDS((n_slots * ns, LANES), U32),
        scratch_shapes=scratch,
        compiler_params=_params(1),
        name="dispatch",
    )(pos, xa, xb)


def _moe_kernel(tile_ref, ex_ref, lo_ref, hi_ref, flag_ref,
                xs_ref, wg_ref, wu_ref, wd_ref, ys_ref,
                wgb_ref, wub_ref, wdb_ref, acc_ref):
    del tile_ref, ex_ref
    w = pl.program_id(0)
    lo = lo_ref[w]
    hi = hi_ref[w]
    flags = flag_ref[w]

    @pl.when((flags & 4) != 0)
    def _():
        wgb_ref[...] = wg_ref[...].astype(BF16)
        wub_ref[...] = wu_ref[...].astype(BF16)
        wdb_ref[...] = wd_ref[...].astype(BF16)

    @pl.when(hi > lo)
    def _():
        x_lo, x_hi = _unpack_halves(_load_token_rows(xs_ref, acc_ref.shape[0]))
        x_lo = x_lo.astype(BF16)
        x_hi = x_hi.astype(BF16)
        half = x_lo.shape[1]
        hg = _dot(x_lo, wgb_ref[0:half, :]) + _dot(x_hi, wgb_ref[half:2 * half, :])
        hu = _dot(x_lo, wub_ref[0:half, :]) + _dot(x_hi, wub_ref[half:2 * half, :])
        rowi = lax.broadcasted_iota(I32, hg.shape, 0)
        mine = jnp.logical_and(rowi >= lo, rowi < hi)
        act = jnp.where(mine, _silu(hg) * hu, 0.0)
        y = _dot(act.astype(BF16), wdb_ref[...])

        @pl.when((flags & 1) != 0)
        def _():
            acc_ref[...] = y

        @pl.when((flags & 1) == 0)
        def _():
            acc_ref[...] += y

    @pl.when((flags & 2) != 0)
    def _():
        _store_token_rows(ys_ref, _pack_halves(acc_ref[...]))


def _moe(xs, work, w_eg, w_eu, w_ed, tm):
    e, d, f = w_eg.shape
    ns = d // 2 // LANES
    n_work = work[0].shape[0]
    row_spec = pl.BlockSpec((tm * ns, LANES), lambda w, tile, ex, lo, hi, fl: (tile[w], 0))
    up_spec = pl.BlockSpec((None, d, f), lambda w, tile, ex, lo, hi, fl: (ex[w], 0, 0))
    dn_spec = pl.BlockSpec((None, f, d), lambda w, tile, ex, lo, hi, fl: (ex[w], 0, 0))
    grid_spec = pltpu.PrefetchScalarGridSpec(
        num_scalar_prefetch=5,
        grid=(n_work,),
        in_specs=[row_spec, up_spec, up_spec, dn_spec],
        out_specs=row_spec,
        scratch_shapes=[pltpu.VMEM((d, f), BF16), pltpu.VMEM((d, f), BF16),
                        pltpu.VMEM((f, d), BF16), pltpu.VMEM((tm, d), F32)],
    )
    return pl.pallas_call(
        _moe_kernel,
        grid_spec=grid_spec,
        out_shape=SDS(xs.shape, U32),
        compiler_params=_params(1),
        name="moe",
    )(*work, xs, w_eg, w_eu, w_ed)


def _final_kernel(pos_ref, posn_ref, w_ref, h_ref, hn_ref, p_ref, ys_ref,
                  wsg_ref, wsu_ref, wsd_ref, gp_ref, wpg_ref, wpp_ref, y_ref,
                  gbuf_ref, sem_ref, *, tt, k):
    s = pl.program_id(0)
    n = pl.num_programs(0)
    slot = lax.rem(s, 2)

    ns = hn_ref.shape[0] // tt

    def issue(idx_ref, sl):
        def body(r, carry):
            row0 = pl.multiple_of(r * ns, ns)
            for kk in range(k):
                src = pl.multiple_of(idx_ref[0, 0, r * k + kk], ns)
                pltpu.make_async_copy(ys_ref.at[pl.ds(src, ns)],
                                      gbuf_ref.at[sl, kk, pl.ds(row0, ns)],
                                      sem_ref.at[sl]).start(priority=kk % 2)
            return carry
        lax.fori_loop(0, tt, body, 0)

    @pl.when(s == 0)
    def _():
        issue(pos_ref, 0)

    @pl.when(s + 1 < n)
    def _():
        issue(posn_ref, 1 - slot)

    for kk in range(k):
        pltpu.make_async_copy(ys_ref.at[pl.ds(0, tt * ns)], gbuf_ref.at[slot, kk],
                              sem_ref.at[slot]).wait()

    wts = w_ref[...]
    r_lo = None
    for kk in range(k):
        lo, hi = _unpack_halves(_load_token_rows(gbuf_ref.at[slot, kk], tt))
        wk = wts[:, kk:kk + 1]
        r_lo = wk * lo if r_lo is None else r_lo + wk * lo
        r_hi = wk * hi if kk == 0 else r_hi + wk * hi
    routed = jnp.concatenate([r_lo, r_hi], axis=-1)

    n_lo, n_hi = _unpack_halves(_load_token_rows(hn_ref, tt))
    n_lo = n_lo.astype(BF16)
    n_hi = n_hi.astype(BF16)
    half = n_lo.shape[1]
    sg = _dot(n_lo, wsg_ref[0:half, :]) + _dot(n_hi, wsg_ref[half:2 * half, :])
    su = _dot(n_lo, wsu_ref[0:half, :]) + _dot(n_hi, wsu_ref[half:2 * half, :])
    shared = _dot((_silu(sg) * su).astype(BF16), wsd_ref[...])
    h2 = h_ref[...] + routed + shared
    gate = jax.nn.sigmoid(_dot(_rms(h2, gp_ref[...]).astype(BF16), wpg_ref[...]))
    y_ref[...] = h2 + gate * _dot(p_ref[...].astype(BF16), wpp_ref[...])


def _final(h, hn_packed, p, ys, pos_tk, w_tk, w_sg, w_su, w_sd, g_ple, w_pg, w_pp, tt):
    t, d = h.shape
    k = pos_tk.shape[1]
    ns = d // 2 // LANES
    tt = min(tt, t)
    assert t % tt == 0
    n = t // tt
    pos = pos_tk.reshape(n, 1, tt * k)
    row = lambda w: pl.BlockSpec((tt, w), lambda s: (s, 0))
    const = lambda a: pl.BlockSpec(a.shape, lambda s: (0, 0), pipeline_mode=pl.Buffered(1))
    smem = lambda fn: pl.BlockSpec((1, 1, tt * k), fn, memory_space=pltpu.SMEM)
    return pl.pallas_call(
        functools.partial(_final_kernel, tt=tt, k=k),
        grid=(n,),
        in_specs=[smem(lambda s: (s, 0, 0)),
                  smem(lambda s: (jnp.minimum(s + 1, n - 1), 0, 0)),
                  row(k), row(d), pl.BlockSpec((tt * ns, LANES), lambda s: (s, 0)),
                  row(p.shape[1]),
                  pl.BlockSpec(memory_space=pl.ANY),
                  const(w_sg), const(w_su), const(w_sd), const(g_ple), const(w_pg),
                  const(w_pp)],
        out_specs=row(d),
        out_shape=SDS((t, d), F32),
        scratch_shapes=[pltpu.VMEM((2, k, tt * ns, LANES), U32),
                        pltpu.SemaphoreType.DMA((2,))],
        compiler_params=_params(1),
        name="final",
    )(pos, pos, w_tk, h, hn_packed, p, ys, w_sg, w_su, w_sd, g_ple, w_pg, w_pp)


MOE_TILE = 512


def kernel(x_prompt, x_sample, p_prompt, p_sample, cache_k, cache_v, state_conv, state_h, page_table, g_mix, w_in, conv_w, conv_b, w_a, b_a, w_i, b_i, lru_lambda, q_gain, k_gain, sb_bias, g_lru_out, g_att_out, w_o, g_ffn, w_router, e_bias, w_eg, w_eu, w_ed, w_sg, w_su, w_sd, g_ple, w_ple_gate, w_ple_proj):
    depth = g_mix.shape[0]
    assert depth == 1
    batch, seq, d = x_prompt.shape
    dec_batch, dec_seq, _ = x_sample.shape
    assert dec_seq == 1
    n_heads, dh = cache_k.shape[3], cache_k.shape[4]
    c = n_heads * dh
    n_taps = conv_w.shape[1]
    assert w_in.shape[2] == 5 * c and seq >= n_taps - 1
    t_p = batch * seq

    row = lambda a: a.reshape(1, -1)
    wai = jnp.concatenate([w_a[0], w_i[0]], axis=-1).astype(BF16)
    lru_w = (conv_w[0], row(conv_b[0]), wai, row(b_a[0]), row(b_i[0]),
             row(lru_lambda[0]), row(g_lru_out[0]))
    in_proj = functools.partial(_in_proj, g_mix=row(g_mix[0]), w_in_bf=w_in[0].astype(BF16),
                                q_gain=row(q_gain[0]), k_gain=row(k_gain[0]),
                                n_heads=n_heads)
    wr_t = w_router[0].T
    wr_hi = wr_t.astype(BF16)
    wr_lo = (wr_t - wr_hi.astype(F32)).astype(BF16)
    out_proj = functools.partial(_out_proj, g_att=row(g_att_out[0]), w_o_bf=w_o[0].astype(BF16),
                                 g_ffn=row(g_ffn[0]),
                                 w_router_t=jnp.concatenate([wr_hi, wr_lo], axis=0))
    final = functools.partial(
        _final, w_sg=w_sg[0].astype(BF16), w_su=w_su[0].astype(BF16),
        w_sd=w_sd[0].astype(BF16), g_ple=row(g_ple[0]), w_pg=w_ple_gate[0].astype(BF16),
        w_pp=w_ple_proj[0].astype(BF16))

    xp = x_prompt.reshape(t_p, d)
    xr, gy, q, k, v, kb, vb = in_proj(xp, tm=512)
    mix_lru, h_last = _lru_prompt(
        xr.reshape(batch, seq, c), gy.reshape(batch, seq, c),
        jnp.zeros((batch, 8, c), F32), jnp.zeros((batch, 1, c), F32), *lru_w, tl=256)
    att = _attn_prompt(q, kb, vb, sb_bias[0], batch, n_heads, tq=512, tk=256, hp=4)
    h_p, hn_p, logits_p = out_proj(mix_lru.reshape(t_p, c), att, xp, tm=256)

    xs_ = x_sample.reshape(dec_batch, d)
    xr_s, gy_s, q_s, k_s, v_s, _, _ = in_proj(xs_, tm=dec_batch)
    mix_lru_s, h_s = _lru_step(xr_s, gy_s, jnp.swapaxes(state_conv[0], 0, 1),
                               state_h[0], *lru_w)
    att_s = _attn_sample(q_s.reshape(dec_batch, n_heads, dh), cache_k[0], cache_v[0],
                         page_table, sb_bias[0], n_pages_step=8)
    h_s2, hn_s, logits_s = out_proj(mix_lru_s, att_s.reshape(dec_batch, c), xs_, tm=dec_batch)

    n_experts = w_router.shape[2]
    ek_p, rk_p, wk_p, counts_p = _router(logits_p, e_bias[0],
                                         jnp.zeros((n_experts, 1), F32), tm=256)
    ek_s, rk_s, wk_s, counts = _router(logits_s, e_bias[0], counts_p, tm=256)
    n_tiles = pl.cdiv((t_p + dec_batch) * TOP_K, MOE_TILE)
    work, offsets = _work_list(counts[:, 0].astype(I32), n_tiles, MOE_TILE)
    ns = d // 2 // LANES
    pos_p = _positions(offsets, ek_p, rk_p, ns, tm=512).T
    pos_s = _positions(offsets, ek_s, rk_s, ns, tm=512).T
    rows = _dispatch(hn_p, hn_s, jnp.concatenate([pos_p, pos_s]), n_tiles * MOE_TILE, ns,
                     tt=128)
    ys = _moe(rows, work, w_eg[0], w_eu[0], w_ed[0], MOE_TILE)
    y_prompt = final(h_p, hn_p, p_prompt[0].reshape(t_p, -1), ys, pos_p, wk_p.T, tt=256)
    y_sample = final(h_s2, hn_s, p_sample[0].reshape(dec_batch, -1), ys, pos_s, wk_s.T,
                     tt=256)

    k_prompt = k.reshape(1, batch, seq, n_heads, dh)
    v_prompt = v.reshape(1, batch, seq, n_heads, dh)
    conv_prompt = xr.reshape(batch, seq, c)[:, seq - (n_taps - 1):][None]
    h_prompt = h_last.reshape(1, batch, c)
    k_sample = k_s.reshape(1, dec_batch, 1, n_heads, dh)
    v_sample = v_s.reshape(1, dec_batch, 1, n_heads, dh)
    conv_sample = jnp.concatenate([state_conv[0][:, 1:], xr_s[:, None]], axis=1)[None]
    h_sample = h_s.reshape(1, dec_batch, c)

    return (y_prompt.reshape(batch, seq, d), y_sample.reshape(dec_batch, 1, d),
            k_prompt, v_prompt, conv_prompt, h_prompt,
            k_sample, v_sample, conv_sample, h_sample)
```

```python
import functools

import jax
import jax.numpy as jnp
from jax import lax
from jax.experimental import pallas as pl
from jax.experimental.pallas import tpu as pltpu

F32 = jnp.float32
BF16 = jnp.bfloat16
U32 = jnp.uint32
I32 = jnp.int32
SDS = jax.ShapeDtypeStruct

EPS = 1e-6
LRU_C = 8.0
N_GROUPS = 8
TOPK_GROUPS = 4
TOP_K = 8
ROUTED_SCALE = 2.5

VMEM_LIMIT_BYTES = 56 * 1024 * 1024

NT_DIMS = (((1,), (1,)), ((), ()))


def _params(n_grid_axes):
    return pltpu.CompilerParams(
        dimension_semantics=("arbitrary",) * n_grid_axes,
        vmem_limit_bytes=VMEM_LIMIT_BYTES,
    )


def _rms(x, g):
    return x * lax.rsqrt(jnp.mean(x * x, axis=-1, keepdims=True) + EPS) * g


def _softplus(x):
    return jnp.maximum(x, 0.0) + jnp.log(1.0 + jnp.exp(-jnp.abs(x)))


def _silu(x):
    return x * jax.nn.sigmoid(x)


def _dot(a, b):
    return jnp.dot(a, b, preferred_element_type=F32)


def _pack_halves(x):
    half = x.shape[1] // 2
    return pltpu.pack_elementwise([x[:, :half], x[:, half:]], packed_dtype=BF16)


LANES = 128


def _store_token_rows(ref, packed):
    m, w = packed.shape
    ns = w // LANES
    for s in range(ns):
        ref[pl.ds(s, m, stride=ns), :] = packed[:, s * LANES:(s + 1) * LANES]


def _load_token_rows(ref, m):
    ns = ref.shape[0] // m
    return jnp.concatenate([ref[pl.ds(s, m, stride=ns), :] for s in range(ns)], axis=-1)


def _unpack_halves(w):
    unpack = functools.partial(pltpu.unpack_elementwise, w, packed_dtype=BF16,
                               unpacked_dtype=F32)
    return unpack(index=0), unpack(index=1)


def _head_rms(z, gain, n_heads):
    dh = z.shape[1] // n_heads
    return jnp.concatenate(
        [_rms(z[:, h * dh:(h + 1) * dh], gain) for h in range(n_heads)], axis=-1)


def _in_proj_kernel(x_ref, g_ref, w_ref, qg_ref, kg_ref,
                    xr_ref, gy_ref, q_ref, k_ref, v_ref, kb_ref, vb_ref, *, n_heads):
    c = xr_ref.shape[1]
    dh = c // n_heads
    xn = _rms(x_ref[...], g_ref[...]).astype(BF16)
    z = lambda j: _dot(xn, w_ref[:, j * c:(j + 1) * c])
    xr_ref[...] = z(0)
    gy_ref[...] = jax.nn.gelu(z(1)).astype(BF16)
    q_ref[...] = (_head_rms(z(2), qg_ref[...], n_heads) * dh ** -0.5).astype(BF16)
    k = _head_rms(z(3), kg_ref[...], n_heads)
    k_ref[...] = k
    kb_ref[...] = k.astype(BF16)
    v = z(4)
    v_ref[...] = v
    vb_ref[...] = v.astype(BF16)


def _in_proj(x, g_mix, w_in_bf, q_gain, k_gain, n_heads, tm):
    t, d = x.shape
    c = w_in_bf.shape[1] // 5
    dh = c // n_heads
    tm = min(tm, t)
    assert t % tm == 0
    row = lambda i: (i, 0)
    const = lambda i: (0, 0)
    out_blk = pl.BlockSpec((tm, c), row)
    return pl.pallas_call(
        functools.partial(_in_proj_kernel, n_heads=n_heads),
        grid=(t // tm,),
        in_specs=[
            pl.BlockSpec((tm, d), row),
            pl.BlockSpec((1, d), const),
            pl.BlockSpec(w_in_bf.shape, const, pipeline_mode=pl.Buffered(1)),
            pl.BlockSpec((1, dh), const),
            pl.BlockSpec((1, dh), const),
        ],
        out_specs=[out_blk] * 7,
        out_shape=[SDS((t, c), F32), SDS((t, c), BF16), SDS((t, c), BF16),
                   SDS((t, c), F32), SDS((t, c), F32), SDS((t, c), BF16),
                   SDS((t, c), BF16)],
        compiler_params=_params(1),
        name="in_proj",
    )(x, g_mix, w_in_bf, q_gain, k_gain)


def _lru_gates(xc, wai_ref, ba, bi, lam):
    n_blocks, blk, _ = wai_ref.shape
    xcb = xc.astype(BF16)
    ga, gi = [], []
    for n in range(n_blocks):
        g = _dot(xcb[:, n * blk:(n + 1) * blk], wai_ref[n])
        ga.append(g[:, :blk])
        gi.append(g[:, blk:])
    r = jax.nn.sigmoid(jnp.concatenate(ga, axis=-1) + ba)
    ig = jax.nn.sigmoid(jnp.concatenate(gi, axis=-1) + bi)
    log_a = -LRU_C * r * _softplus(-lam)
    a = jnp.exp(log_a)
    th = jnp.tanh(log_a)
    u = jnp.sqrt(-2.0 * th / (1.0 - th)) * ig * xc
    return a, u


def _lru_prompt_kernel(xr_ref, gy_ref, st_ref, h0_ref, cw_ref, cb_ref, wai_ref,
                       ba_ref, bi_ref, lam_ref, gl_ref, mix_ref, ht_ref,
                       xext_ref, a_ref, u_ref, h_ref, *, tl):
    l = pl.program_id(1)

    @pl.when(l == 0)
    def _():
        xext_ref[0:8, :] = st_ref[0]
        h_ref[...] = h0_ref[0]

    xext_ref[8:8 + tl, :] = xr_ref[0]
    cw = cw_ref[...]
    n_taps = cw.shape[0]
    xc = cb_ref[...]
    for j in range(n_taps):
        off = 8 - (n_taps - 1) + j
        xc = xc + cw[j:j + 1, :] * xext_ref[off:off + tl, :]
    xext_ref[0:8, :] = xext_ref[tl:tl + 8, :]

    a, u = _lru_gates(xc, wai_ref, ba_ref[...], bi_ref[...], lam_ref[...])
    a_ref[...] = a
    u_ref[...] = u

    def step(t, h):
        h = a_ref[pl.ds(t, 1), :] * h + u_ref[pl.ds(t, 1), :]
        u_ref[pl.ds(t, 1), :] = h
        return h

    h = lax.fori_loop(0, tl, step, h_ref[...], unroll=8)
    h_ref[...] = h
    mix_ref[0] = _rms(u_ref[...] * gy_ref[0], gl_ref[...]).astype(BF16)

    @pl.when(l == pl.num_programs(1) - 1)
    def _():
        ht_ref[0] = h


def _lru_prompt(xr, gy, state8, h0, conv_w, conv_b, wai, b_a, b_i, lam, g_lru, tl):
    b, l, c = xr.shape
    tl = min(tl, l)
    assert l % tl == 0 and tl % 8 == 0
    tile = pl.BlockSpec((1, tl, c), lambda i, j: (i, j, 0))
    per_b = lambda rows: pl.BlockSpec((1, rows, c), lambda i, j: (i, 0, 0))
    vec = pl.BlockSpec((1, c), lambda i, j: (0, 0))
    return pl.pallas_call(
        functools.partial(_lru_prompt_kernel, tl=tl),
        grid=(b, l // tl),
        in_specs=[tile, tile, per_b(8), per_b(1),
                  pl.BlockSpec(conv_w.shape, lambda i, j: (0, 0)), vec,
                  pl.BlockSpec(wai.shape, lambda i, j: (0, 0, 0)),
                  vec, vec, vec, vec],
        out_specs=[tile, per_b(1)],
        out_shape=[SDS((b, l, c), BF16), SDS((b, 1, c), F32)],
        scratch_shapes=[pltpu.VMEM((tl + 8, c), F32), pltpu.VMEM((tl, c), F32),
                        pltpu.VMEM((tl, c), F32), pltpu.VMEM((1, c), F32)],
        compiler_params=_params(2),
        name="lru_prompt",
    )(xr, gy, state8, h0, conv_w, conv_b, wai, b_a, b_i, lam, g_lru)


def _lru_step_kernel(xr_ref, gy_ref, st_ref, h0_ref, cw_ref, cb_ref, wai_ref,
                     ba_ref, bi_ref, lam_ref, gl_ref, mix_ref, hn_ref):
    cw = cw_ref[...]
    n_taps = cw.shape[0]
    xc = cb_ref[...] + cw[n_taps - 1:n_taps, :] * xr_ref[...]
    for j in range(n_taps - 1):
        xc = xc + cw[j:j + 1, :] * st_ref[j]
    a, u = _lru_gates(xc, wai_ref, ba_ref[...], bi_ref[...], lam_ref[...])
    h = a * h0_ref[...] + u
    hn_ref[...] = h
    mix_ref[...] = _rms(h * gy_ref[...], gl_ref[...]).astype(BF16)


def _lru_step(xr, gy, state, h0, conv_w, conv_b, wai, b_a, b_i, lam, g_lru):
    b, c = xr.shape
    return pl.pallas_call(
        _lru_step_kernel,
        out_shape=[SDS((b, c), BF16), SDS((b, c), F32)],
        compiler_params=pltpu.CompilerParams(vmem_limit_bytes=VMEM_LIMIT_BYTES),
        name="lru_step",
    )(xr, gy, state, h0, conv_w, conv_b, wai, b_a, b_i, lam, g_lru)


def _attn_prompt_kernel(bias_ref, q_ref, k_ref, v_ref, tri_ref, o_ref, *, tq, tk, hp):
    hg = pl.program_id(1)
    i = pl.program_id(2)
    tri = tri_ref[...]
    dh = q_ref.shape[1] // hp
    r = tq // tk
    qs = [q_ref[:, a * dh:(a + 1) * dh] for a in range(hp)]
    biases = [bias_ref[hg * hp + a] for a in range(hp)]
    row = lax.broadcasted_iota(I32, (tq, tk), 0)
    col = lax.broadcasted_iota(I32, (tq, tk), 1)

    def tile(j, carry, diag):
        start = pl.multiple_of(j * tk, tk)
        new = []
        for a in range(hp):
            c, acc = carry[a]
            kj = k_ref[pl.ds(start, tk), a * dh:(a + 1) * dh]
            vj = v_ref[pl.ds(start, tk), a * dh:(a + 1) * dh]
            z = lax.dot_general(qs[a], kj, NT_DIMS, preferred_element_type=F32) + biases[a]
            sp = _softplus(z)
            if diag is not None:
                causal = col + diag * tk < row
                sp = jnp.where(causal, sp, 0.0)
            drop = _dot(sp.astype(BF16), tri)
            w = jnp.exp(z - c - drop)
            if diag is not None:
                w = jnp.where(causal, w, 0.0)
            acc = acc + _dot(w.astype(BF16), vj)
            c = c + drop[:, 0:1]
            new.append((c, acc))
        return tuple(new)

    carry = tuple((jnp.zeros((tq, 1), F32), jnp.zeros((tq, dh), F32)) for _ in range(hp))
    for dj in reversed(range(r)):
        carry = tile(i * r + dj, carry, dj)
    carry = lax.fori_loop(0, i * r, lambda jj, cr: tile(i * r - 1 - jj, cr, None), carry)
    o_ref[...] = jnp.concatenate([carry[a][1] for a in range(hp)], axis=-1)


def _attn_prompt(q, k, v, sb_bias, batch, n_heads, tq, tk, hp):
    t, c = q.shape
    l = t // batch
    dh = c // n_heads
    tq = min(tq, l)
    tk = min(tk, tq)
    hp = min(hp, n_heads)
    assert l % tq == 0 and tq % tk == 0 and n_heads % hp == 0
    nq = l // tq
    tri = (jnp.arange(tk)[:, None] >= jnp.arange(tk)[None, :]).astype(BF16)
    kv_spec = pl.BlockSpec((l, hp * dh), lambda b, h, i: (b, h))
    q_spec = pl.BlockSpec((tq, hp * dh), lambda b, h, i: (b * nq + i, h))
    return pl.pallas_call(
        functools.partial(_attn_prompt_kernel, tq=tq, tk=tk, hp=hp),
        grid=(batch, n_heads // hp, nq),
        in_specs=[pl.BlockSpec(memory_space=pltpu.SMEM), q_spec, kv_spec, kv_spec,
                  pl.BlockSpec((tk, tk), lambda b, h, i: (0, 0))],
        out_specs=q_spec,
        out_shape=SDS((t, c), F32),
        compiler_params=_params(3),
        name="attn_prompt",
    )(sb_bias, q, k, v, tri)


def _shift_lanes_left(x, s):
    n = x.shape[1]
    col = lax.broadcasted_iota(I32, x.shape, 1)
    return jnp.where(col < n - s, pltpu.roll(x, n - s, axis=1), 0.0)


def _shift_rows_down(x, s):
    rowi = lax.broadcasted_iota(I32, x.shape, 0)
    return jnp.where(rowi >= s, pltpu.roll(x, s, axis=0), 0.0)


def _attn_sample_kernel(pt_ref, bias_ref, q_ref, *refs, n_pages_step):
    del pt_ref
    g_ = n_pages_step
    k_refs = refs[:g_]
    v_refs = refs[g_:2 * g_]
    o_ref, c_ref, acc_ref = refs[2 * g_:]
    step = pl.program_id(1)

    @pl.when(step == 0)
    def _():
        c_ref[...] = jnp.zeros_like(c_ref)
        acc_ref[...] = jnp.zeros_like(acc_ref)

    q = q_ref[0]
    n_heads, dh = q.shape
    page = k_refs[0].shape[0]
    n_col = page * n_heads
    row = lax.broadcasted_iota(I32, (n_heads, n_col), 0)
    col = lax.broadcasted_iota(I32, (n_heads, n_col), 1)
    own = lax.rem(col, n_heads) == row
    pick = lambda t: jnp.sum(jnp.where(own, t, 0.0), axis=0, keepdims=True)
    bias_row = pick(jnp.broadcast_to(bias_ref[...], (n_heads, n_col)))
    prow = lax.broadcasted_iota(I32, (g_, n_col), 0)

    z = jnp.zeros((g_, n_col), F32)
    for g in range(g_):
        kp = k_refs[g][...].reshape(n_col, dh).astype(BF16)
        z8 = lax.dot_general(q, kp, NT_DIMS, preferred_element_type=F32)
        z = jnp.where(prow == g, jnp.broadcast_to(pick(z8), (g_, n_col)), z)
    z = z + bias_row
    sp = _softplus(z)
    log_keep = -sp
    log_after = _shift_lanes_left(log_keep, n_heads)
    total = log_keep
    s = n_heads
    while s < n_col:
        log_after = log_after + _shift_lanes_left(log_after, s)
        total = total + pltpu.roll(total, s, axis=1)
        s *= 2
    before = _shift_rows_down(total, 1)
    s = 1
    while s < g_:
        before = before + _shift_rows_down(before, s)
        s *= 2
    c_in = c_ref[...]
    w = jnp.exp(z - sp + log_after + before + c_in)
    c_ref[...] = c_in + jnp.sum(total, axis=0, keepdims=True)

    acc = acc_ref[...]
    for g in range(g_):
        w8 = jnp.where(own, jnp.broadcast_to(w[g:g + 1, :], (n_heads, n_col)), 0.0)
        vp = v_refs[g][...].reshape(n_col, dh).astype(BF16)
        acc = acc + _dot(w8.astype(BF16), vp)
    acc_ref[...] = acc

    @pl.when(step == pl.num_programs(1) - 1)
    def _():
        o_ref[0] = acc


def _attn_sample(q, cache_k, cache_v, page_table, sb_bias, n_pages_step):
    b, n_heads, dh = q.shape
    _, page, _, _ = cache_k.shape
    n_pages = page_table.shape[1]
    g_ = min(n_pages_step, n_pages)
    assert n_pages % g_ == 0 and page & (page - 1) == 0

    def page_spec(g):
        return pl.BlockSpec(
            (None, page, n_heads, dh),
            lambda i, j, pt: (pt[i, n_pages - 1 - (j * g_ + g)], 0, 0, 0))

    q_spec = pl.BlockSpec((1, n_heads, dh), lambda i, j, pt: (i, 0, 0))
    grid_spec = pltpu.PrefetchScalarGridSpec(
        num_scalar_prefetch=1,
        grid=(b, n_pages // g_),
        in_specs=[pl.BlockSpec((n_heads, 1), lambda i, j, pt: (0, 0)), q_spec]
        + [page_spec(g) for g in range(g_)] * 2,
        out_specs=q_spec,
        scratch_shapes=[pltpu.VMEM((1, page * n_heads), F32),
                        pltpu.VMEM((n_heads, dh), F32)],
    )
    return pl.pallas_call(
        functools.partial(_attn_sample_kernel, n_pages_step=g_),
        grid_spec=grid_spec,
        out_shape=SDS((b, n_heads, dh), F32),
        compiler_params=_params(2),
        name="attn_sample",
    )(page_table, sb_bias.reshape(n_heads, 1), q,
      *([cache_k] * g_), *([cache_v] * g_))


def _out_proj_kernel(ml_ref, att_ref, x_ref, ga_ref, wo_ref, gf_ref, wr_ref,
                     h_ref, hn_ref, lg_ref):
    c = ml_ref.shape[1]
    attn = _rms(att_ref[...], ga_ref[...]).astype(BF16)
    mix = _dot(ml_ref[...], wo_ref[0:c, :]) + _dot(attn, wo_ref[c:2 * c, :])
    h = x_ref[...] + mix
    h_ref[...] = h
    hn = _rms(h, gf_ref[...])
    _store_token_rows(hn_ref, _pack_halves(hn))
    e = lg_ref.shape[0]
    hn_hi = hn.astype(BF16)
    hn_lo = (hn - hn_hi.astype(F32)).astype(BF16)
    a = lax.dot_general(wr_ref[...], hn_hi, NT_DIMS, preferred_element_type=F32)
    b = lax.dot_general(wr_ref[0:e, :], hn_lo, NT_DIMS, preferred_element_type=F32)
    lg_ref[...] = a[0:e, :] + a[e:2 * e, :] + b


def _out_proj(mix_lru, att, x, g_att, w_o_bf, g_ffn, w_router_t, tm):
    t, d = x.shape
    c = mix_lru.shape[1]
    e = w_router_t.shape[0] // 2
    ns = d // 2 // LANES
    tm = min(tm, t)
    assert t % tm == 0
    row = lambda w: pl.BlockSpec((tm, w), lambda i: (i, 0))
    full = lambda a: pl.BlockSpec(a.shape, lambda i: (0, 0))
    return pl.pallas_call(
        _out_proj_kernel,
        grid=(t // tm,),
        in_specs=[row(c), row(c), row(d), full(g_att), full(w_o_bf), full(g_ffn),
                  full(w_router_t)],
        out_specs=[row(d), pl.BlockSpec((tm * ns, LANES), lambda i: (i, 0)),
                   pl.BlockSpec((e, tm), lambda i: (0, i))],
        out_shape=[SDS((t, d), F32), SDS((t * ns, LANES), U32), SDS((e, t), F32)],
        compiler_params=_params(1),
        name="out_proj",
    )(mix_lru, att, x, g_att, w_o_bf, g_ffn, w_router_t)


def _take_top(cur, idx, sentinel, n):
    picked = jnp.zeros(cur.shape, jnp.bool_)
    for _ in range(n):
        m = jnp.max(cur, axis=0, keepdims=True)
        first = jnp.min(jnp.where(cur == m, idx, sentinel), axis=0, keepdims=True)
        pick = idx == first
        picked = jnp.logical_or(picked, pick)
        cur = jnp.where(pick, -jnp.inf, cur)
    return picked


def _router_kernel(lg_ref, eb_ref, cin_ref, ek_ref, rk_ref, wk_ref, cnt_ref, run_ref):
    e, tm = lg_ref.shape
    per = e // N_GROUPS

    @pl.when(pl.program_id(0) == 0)
    def _():
        run_ref[...] = cin_ref[...]

    scores = jax.nn.sigmoid(lg_ref[...])
    biased = scores + eb_ref[...]
    sub = lax.broadcasted_iota(I32, (per, tm), 0)
    group_scores = []
    for g in range(N_GROUPS):
        bg = biased[g * per:(g + 1) * per, :]
        m1 = jnp.max(bg, axis=0, keepdims=True)
        first = jnp.min(jnp.where(bg == m1, sub, per), axis=0, keepdims=True)
        m2 = jnp.max(jnp.where(sub == first, -jnp.inf, bg), axis=0, keepdims=True)
        group_scores.append(m1 + m2)
    gs = jnp.concatenate(group_scores, axis=0)
    gidx = lax.broadcasted_iota(I32, gs.shape, 0)
    gsel = _take_top(gs, gidx, N_GROUPS, TOPK_GROUPS)
    emask = jnp.concatenate(
        [jnp.broadcast_to(gsel[g:g + 1, :], (per, tm)) for g in range(N_GROUPS)],
        axis=0)
    eidx = lax.broadcasted_iota(I32, (e, tm), 0)
    sel = _take_top(jnp.where(emask, biased, -jnp.inf), eidx, e, TOP_K)
    chosen = jnp.where(sel, scores, 0.0)
    gate = chosen / jnp.sum(chosen, axis=0, keepdims=True) * ROUTED_SCALE

    picked = jnp.where(sel, 1.0, 0.0)
    er = lax.broadcasted_iota(I32, (e, e), 0)
    ec = lax.broadcasted_iota(I32, (e, e), 1)
    slot = _dot(jnp.where(ec < er, 1.0, 0.0), picked)
    tr = lax.broadcasted_iota(I32, (tm, tm), 0)
    tc = lax.broadcasted_iota(I32, (tm, tm), 1)
    rank = _dot(picked, jnp.where(tr < tc, 1.0, 0.0)) + run_ref[...]
    run_ref[...] = run_ref[...] + jnp.sum(picked, axis=1, keepdims=True)
    cnt_ref[...] = run_ref[...]

    eidx_f = eidx.astype(F32)
    col = lambda m, v: jnp.sum(jnp.where(m, v, 0.0), axis=0, keepdims=True)
    eks, rks, wks = [], [], []
    for k in range(TOP_K):
        m = jnp.logical_and(sel, slot == k)
        eks.append(col(m, eidx_f))
        rks.append(col(m, rank))
        wks.append(col(m, gate))
    ek_ref[...] = jnp.concatenate(eks, axis=0).astype(I32)
    rk_ref[...] = jnp.concatenate(rks, axis=0).astype(I32)
    wk_ref[...] = jnp.concatenate(wks, axis=0)


def _router(logits_t, e_bias, counts_in, tm):
    e, t = logits_t.shape
    tm = min(tm, t)
    assert t % tm == 0
    blk = pl.BlockSpec((e, tm), lambda i: (0, i))
    vec = pl.BlockSpec((e, 1), lambda i: (0, 0))
    slot_blk = pl.BlockSpec((TOP_K, tm), lambda i: (0, i))
    return pl.pallas_call(
        _router_kernel,
        grid=(t // tm,),
        in_specs=[blk, vec, vec],
        out_specs=[slot_blk, slot_blk, slot_blk, vec],
        out_shape=[SDS((TOP_K, t), I32), SDS((TOP_K, t), I32), SDS((TOP_K, t), F32),
                   SDS((e, 1), F32)],
        scratch_shapes=[pltpu.VMEM((e, 1), F32)],
        compiler_params=_params(1),
        name="router",
    )(logits_t, e_bias.reshape(e, 1), counts_in)


def _positions_kernel(off_ref, ek_ref, rk_ref, pos_ref, *, ns):
    ek = ek_ref[...]
    pos = rk_ref[...]
    for e in range(off_ref.shape[0]):
        pos = pos + jnp.where(ek == e, off_ref[e], 0)
    pos_ref[...] = pos * ns


def _positions(offsets, ek, rk, ns, tm):
    k, t = ek.shape
    tm = min(tm, t)
    assert t % tm == 0
    blk = pl.BlockSpec((k, tm), lambda i: (0, i))
    return pl.pallas_call(
        functools.partial(_positions_kernel, ns=ns),
        grid=(t // tm,),
        in_specs=[pl.BlockSpec(memory_space=pltpu.SMEM), blk, blk],
        out_specs=blk,
        out_shape=SDS((k, t), I32),
        compiler_params=_params(1),
        name="positions",
    )(offsets, ek, rk)


def _work_list(counts, n_tiles, tm):
    e = counts.shape[0]
    ends = jnp.cumsum(counts)
    offsets = ends - counts
    n_work = n_tiles + e
    first_tile = offsets // tm
    n_e = jnp.where(counts > 0, (ends - 1) // tm - first_tile + 1, 0)
    w_end = jnp.cumsum(n_e)
    w_start = w_end - n_e
    total = w_end[-1]
    w = jnp.arange(n_work, dtype=I32)
    valid = w < total
    wc = jnp.minimum(w, total - 1)
    ex = jnp.minimum(jnp.sum((w_end[None, :] <= wc[:, None]).astype(I32), axis=1), e - 1)
    onehot = ex[:, None] == jnp.arange(e, dtype=I32)[None, :]
    take = lambda v: jnp.sum(jnp.where(onehot, v[None, :], 0), axis=1)
    tile = take(first_tile) + (wc - take(w_start))
    lo = jnp.where(valid, jnp.maximum(take(offsets), tile * tm) - tile * tm, 0)
    hi = jnp.where(valid, jnp.minimum(take(ends), (tile + 1) * tm) - tile * tm, 0)
    prev_tile = jnp.concatenate([jnp.full((1,), -1, I32), tile[:-1]])
    next_tile = jnp.concatenate([tile[1:], jnp.full((1,), -1, I32)])
    prev_ex = jnp.concatenate([jnp.full((1,), -1, I32), ex[:-1]])
    first = valid & (tile != prev_tile)
    last = valid & ((tile != next_tile) | (w == total - 1))
    new_ex = valid & (ex != prev_ex)
    flags = first.astype(I32) + 2 * last.astype(I32) + 4 * new_ex.astype(I32)
    return (tile.astype(I32), ex, lo.astype(I32), hi.astype(I32), flags), offsets


def _dispatch_kernel(pos_ref, xa_ref, xb_ref, xs_ref, stage_ref, sem_ref, *pad, k, ns):
    tt = xa_ref.shape[0] // ns
    tb = xb_ref.shape[0] // ns
    s = pl.program_id(0)
    n = pl.num_programs(0)
    slot = lax.rem(s, 2)

    if pad:
        zero_ref, zsem_ref = pad
        n_pad = zero_ref.shape[0]
        fill = pltpu.make_async_copy(
            zero_ref, xs_ref.at[pl.ds(xs_ref.shape[0] - n_pad, n_pad)], zsem_ref.at[0])

        @pl.when(s == 0)
        def _():
            zero_ref[...] = jnp.zeros_like(zero_ref)
            fill.start()

        @pl.when(s == n - 1)
        def _():
            fill.wait()

    def wait_slot(sl, tokens):
        rows = tokens * ns
        for _ in range(k):
            pltpu.make_async_copy(stage_ref.at[sl, pl.ds(0, rows)], xs_ref.at[pl.ds(0, rows)],
                                  sem_ref.at[sl]).wait()

    def issue(tokens):
        def body(r, carry):
            src = stage_ref.at[slot, pl.ds(pl.multiple_of(r * ns, ns), ns)]
            for kk in range(k):
                dst = pl.multiple_of(pos_ref[0, 0, r * k + kk], ns)
                pltpu.make_async_copy(src, xs_ref.at[pl.ds(dst, ns)],
                                      sem_ref.at[slot]).start(priority=kk % 2)
            return carry
        lax.fori_loop(0, tokens, body, 0)

    @pl.when(s >= 2)
    def _():
        wait_slot(slot, tt)

    @pl.when(s < n - 1)
    def _():
        stage_ref[slot] = xa_ref[...]
        issue(tt)

    @pl.when(s == n - 1)
    def _():
        stage_ref[slot, 0:tb * ns, :] = xb_ref[...]
        issue(tb)
        wait_slot(slot, tb)
        wait_slot(1 - slot, tt)


def _dispatch(xa, xb, pos_tk, n_slots, ns, tt):
    ta = xa.shape[0] // ns
    tb = xb.shape[0] // ns
    k = pos_tk.shape[1]
    tt = min(tt, ta)
    assert ta % tt == 0 and tb <= tt
    n_a = ta // tt
    pos_b = jnp.zeros((tt * k,), I32).at[:tb * k].set(pos_tk[ta:].reshape(-1))
    pos = jnp.concatenate([pos_tk[:ta].reshape(-1), pos_b]).reshape(n_a + 1, 1, tt * k)
    n_pad = n_slots - (ta + tb) * k
    scratch = [pltpu.VMEM((2, tt * ns, LANES), U32), pltpu.SemaphoreType.DMA((2,))]
    if n_pad:
        scratch += [pltpu.VMEM((n_pad * ns, LANES), U32), pltpu.SemaphoreType.DMA((1,))]
    return pl.pallas_call(
        functools.partial(_dispatch_kernel, k=k, ns=ns),
        grid=(n_a + 1,),
        in_specs=[pl.BlockSpec((1, 1, tt * k), lambda s: (s, 0, 0), memory_space=pltpu.SMEM),
                  pl.BlockSpec((tt * ns, LANES), lambda s: (jnp.minimum(s, n_a - 1), 0)),
                  pl.BlockSpec((tb * ns, LANES), lambda s: (0, 0))],
        out_specs=pl.BlockSpec(memory_space=pl.ANY),
        out_shape=SDS((n_slots * ns, LANES), U32),
        scratch_shapes=scratch,
        compiler_params=_params(1),
        name="dispatch",
    )(pos, xa, xb)


def _moe_kernel(tile_ref, ex_ref, lo_ref, hi_ref, flag_ref,
                xs_ref, wg_ref, wu_ref, wd_ref, ys_ref,
                wgb_ref, wub_ref, wdb_ref, acc_ref):
    del tile_ref, ex_ref
    w = pl.program_id(0)
    lo = lo_ref[w]
    hi = hi_ref[w]
    flags = flag_ref[w]

    @pl.when((flags & 4) != 0)
    def _():
        wgb_ref[...] = wg_ref[...].astype(BF16)
        wub_ref[...] = wu_ref[...].astype(BF16)
        wdb_ref[...] = wd_ref[...].astype(BF16)

    @pl.when(hi > lo)
    def _():
        x_lo, x_hi = _unpack_halves(_load_token_rows(xs_ref, acc_ref.shape[0]))
        x_lo = x_lo.astype(BF16)
        x_hi = x_hi.astype(BF16)
        half = x_lo.shape[1]
        hg = _dot(x_lo, wgb_ref[0:half, :]) + _dot(x_hi, wgb_ref[half:2 * half, :])
        hu = _dot(x_lo, wub_ref[0:half, :]) + _dot(x_hi, wub_ref[half:2 * half, :])
        rowi = lax.broadcasted_iota(I32, hg.shape, 0)
        mine = jnp.logical_and(rowi >= lo, rowi < hi)
        act = jnp.where(mine, _silu(hg) * hu, 0.0)
        y = _dot(act.astype(BF16), wdb_ref[...])

        @pl.when((flags & 3) == 3)
        def _():
            _store_token_rows(ys_ref, _pack_halves(y))

        @pl.when((flags & 3) == 1)
        def _():
            acc_ref[...] = y

        @pl.when((flags & 1) == 0)
        def _():
            acc_ref[...] += y

    @pl.when((flags & 3) == 2)
    def _():
        _store_token_rows(ys_ref, _pack_halves(acc_ref[...]))


def _moe(xs, work, w_eg, w_eu, w_ed, tm):
    e, d, f = w_eg.shape
    ns = d // 2 // LANES
    n_work = work[0].shape[0]
    row_spec = pl.BlockSpec((tm * ns, LANES), lambda w, tile, ex, lo, hi, fl: (tile[w], 0))
    up_spec = pl.BlockSpec((None, d, f), lambda w, tile, ex, lo, hi, fl: (ex[w], 0, 0))
    dn_spec = pl.BlockSpec((None, f, d), lambda w, tile, ex, lo, hi, fl: (ex[w], 0, 0))
    grid_spec = pltpu.PrefetchScalarGridSpec(
        num_scalar_prefetch=5,
        grid=(n_work,),
        in_specs=[row_spec, up_spec, up_spec, dn_spec],
        out_specs=row_spec,
        scratch_shapes=[pltpu.VMEM((d, f), BF16), pltpu.VMEM((d, f), BF16),
                        pltpu.VMEM((f, d), BF16), pltpu.VMEM((tm, d), F32)],
    )
    return pl.pallas_call(
        _moe_kernel,
        grid_spec=grid_spec,
        out_shape=SDS(xs.shape, U32),
        compiler_params=_params(1),
        name="moe",
    )(*work, xs, w_eg, w_eu, w_ed)


def _final_kernel(pos_ref, posn_ref, w_ref, h_ref, hn_ref, p_ref, ys_ref,
                  wsg_ref, wsu_ref, wsd_ref, gp_ref, wpg_ref, wpp_ref, y_ref,
                  gbuf_ref, sem_ref, *, tt, k):
    s = pl.program_id(0)
    n = pl.num_programs(0)
    slot = lax.rem(s, 2)

    ns = hn_ref.shape[0] // tt

    def issue(idx_ref, sl):
        def body(r, carry):
            row0 = pl.multiple_of(r * ns, ns)
            for kk in range(k):
                src = pl.multiple_of(idx_ref[0, 0, r * k + kk], ns)
                pltpu.make_async_copy(ys_ref.at[pl.ds(src, ns)],
                                      gbuf_ref.at[sl, kk, pl.ds(row0, ns)],
                                      sem_ref.at[sl]).start(priority=kk % 2)
            return carry
        lax.fori_loop(0, tt, body, 0)

    @pl.when(s == 0)
    def _():
        issue(pos_ref, 0)

    @pl.when(s + 1 < n)
    def _():
        issue(posn_ref, 1 - slot)

    for kk in range(k):
        pltpu.make_async_copy(ys_ref.at[pl.ds(0, tt * ns)], gbuf_ref.at[slot, kk],
                              sem_ref.at[slot]).wait()

    wts = w_ref[...]
    r_lo = None
    for kk in range(k):
        lo, hi = _unpack_halves(_load_token_rows(gbuf_ref.at[slot, kk], tt))
        wk = wts[:, kk:kk + 1]
        r_lo = wk * lo if r_lo is None else r_lo + wk * lo
        r_hi = wk * hi if kk == 0 else r_hi + wk * hi
    routed = jnp.concatenate([r_lo, r_hi], axis=-1)

    n_lo, n_hi = _unpack_halves(_load_token_rows(hn_ref, tt))
    n_lo = n_lo.astype(BF16)
    n_hi = n_hi.astype(BF16)
    half = n_lo.shape[1]
    sg = _dot(n_lo, wsg_ref[0:half, :]) + _dot(n_hi, wsg_ref[half:2 * half, :])
    su = _dot(n_lo, wsu_ref[0:half, :]) + _dot(n_hi, wsu_ref[half:2 * half, :])
    shared = _dot((_silu(sg) * su).astype(BF16), wsd_ref[...])
    h2 = h_ref[...] + routed + shared
    gate = jax.nn.sigmoid(_dot(_rms(h2, gp_ref[...]).astype(BF16), wpg_ref[...]))
    y_ref[...] = h2 + gate * _dot(p_ref[...].astype(BF16), wpp_ref[...])


def _final(h, hn_packed, p, ys, pos_tk, w_tk, w_sg, w_su, w_sd, g_ple, w_pg, w_pp, tt):
    t, d = h.shape
    k = pos_tk.shape[1]
    ns = d // 2 // LANES
    tt = min(tt, t)
    assert t % tt == 0
    n = t // tt
    pos = pos_tk.reshape(n, 1, tt * k)
    row = lambda w: pl.BlockSpec((tt, w), lambda s: (s, 0))
    const = lambda a: pl.BlockSpec(a.shape, lambda s: (0, 0), pipeline_mode=pl.Buffered(1))
    smem = lambda fn: pl.BlockSpec((1, 1, tt * k), fn, memory_space=pltpu.SMEM)
    return pl.pallas_call(
        functools.partial(_final_kernel, tt=tt, k=k),
        grid=(n,),
        in_specs=[smem(lambda s: (s, 0, 0)),
                  smem(lambda s: (jnp.minimum(s + 1, n - 1), 0, 0)),
                  row(k), row(d), pl.BlockSpec((tt * ns, LANES), lambda s: (s, 0)),
                  row(p.shape[1]),
                  pl.BlockSpec(memory_space=pl.ANY),
                  const(w_sg), const(w_su), const(w_sd), const(g_ple), const(w_pg),
                  const(w_pp)],
        out_specs=row(d),
        out_shape=SDS((t, d), F32),
        scratch_shapes=[pltpu.VMEM((2, k, tt * ns, LANES), U32),
                        pltpu.SemaphoreType.DMA((2,))],
        compiler_params=_params(1),
        name="final",
    )(pos, pos, w_tk, h, hn_packed, p, ys, w_sg, w_su, w_sd, g_ple, w_pg, w_pp)


MOE_TILE = 512


def kernel(x_prompt, x_sample, p_prompt, p_sample, cache_k, cache_v, state_conv, state_h, page_table, g_mix, w_in, conv_w, conv_b, w_a, b_a, w_i, b_i, lru_lambda, q_gain, k_gain, sb_bias, g_lru_out, g_att_out, w_o, g_ffn, w_router, e_bias, w_eg, w_eu, w_ed, w_sg, w_su, w_sd, g_ple, w_ple_gate, w_ple_proj):
    depth = g_mix.shape[0]
    assert depth == 1
    batch, seq, d = x_prompt.shape
    dec_batch, dec_seq, _ = x_sample.shape
    assert dec_seq == 1
    n_heads, dh = cache_k.shape[3], cache_k.shape[4]
    c = n_heads * dh
    n_taps = conv_w.shape[1]
    assert w_in.shape[2] == 5 * c and seq >= n_taps - 1
    t_p = batch * seq

    row = lambda a: a.reshape(1, -1)
    wai = jnp.concatenate([w_a[0], w_i[0]], axis=-1).astype(BF16)
    lru_w = (conv_w[0], row(conv_b[0]), wai, row(b_a[0]), row(b_i[0]),
             row(lru_lambda[0]), row(g_lru_out[0]))
    in_proj = functools.partial(_in_proj, g_mix=row(g_mix[0]), w_in_bf=w_in[0].astype(BF16),
                                q_gain=row(q_gain[0]), k_gain=row(k_gain[0]),
                                n_heads=n_heads)
    wr_t = w_router[0].T
    wr_hi = wr_t.astype(BF16)
    wr_lo = (wr_t - wr_hi.astype(F32)).astype(BF16)
    out_proj = functools.partial(_out_proj, g_att=row(g_att_out[0]), w_o_bf=w_o[0].astype(BF16),
                                 g_ffn=row(g_ffn[0]),
                                 w_router_t=jnp.concatenate([wr_hi, wr_lo], axis=0))
    final = functools.partial(
        _final, w_sg=w_sg[0].astype(BF16), w_su=w_su[0].astype(BF16),
        w_sd=w_sd[0].astype(BF16), g_ple=row(g_ple[0]), w_pg=w_ple_gate[0].astype(BF16),
        w_pp=w_ple_proj[0].astype(BF16))

    xp = x_prompt.reshape(t_p, d)
    xr, gy, q, k, v, kb, vb = in_proj(xp, tm=512)
    mix_lru, h_last = _lru_prompt(
        xr.reshape(batch, seq, c), gy.reshape(batch, seq, c),
        jnp.zeros((batch, 8, c), F32), jnp.zeros((batch, 1, c), F32), *lru_w, tl=256)
    att = _attn_prompt(q, kb, vb, sb_bias[0], batch, n_heads, tq=512, tk=256, hp=4)
    h_p, hn_p, logits_p = out_proj(mix_lru.reshape(t_p, c), att, xp, tm=256)

    xs_ = x_sample.reshape(dec_batch, d)
    xr_s, gy_s, q_s, k_s, v_s, _, _ = in_proj(xs_, tm=dec_batch)
    mix_lru_s, h_s = _lru_step(xr_s, gy_s, jnp.swapaxes(state_conv[0], 0, 1),
                               state_h[0], *lru_w)
    att_s = _attn_sample(q_s.reshape(dec_batch, n_heads, dh), cache_k[0], cache_v[0],
                         page_table, sb_bias[0], n_pages_step=16)
    h_s2, hn_s, logits_s = out_proj(mix_lru_s, att_s.reshape(dec_batch, c), xs_, tm=dec_batch)

    n_experts = w_router.shape[2]
    ek_p, rk_p, wk_p, counts_p = _router(logits_p, e_bias[0],
                                         jnp.zeros((n_experts, 1), F32), tm=256)
    ek_s, rk_s, wk_s, counts = _router(logits_s, e_bias[0], counts_p, tm=256)
    n_tiles = pl.cdiv((t_p + dec_batch) * TOP_K, MOE_TILE)
    work, offsets = _work_list(counts[:, 0].astype(I32), n_tiles, MOE_TILE)
    ns = d // 2 // LANES
    pos_p = _positions(offsets, ek_p, rk_p, ns, tm=512).T
    pos_s = _positions(offsets, ek_s, rk_s, ns, tm=512).T
    rows = _dispatch(hn_p, hn_s, jnp.concatenate([pos_p, pos_s]), n_tiles * MOE_TILE, ns,
                     tt=128)
    ys = _moe(rows, work, w_eg[0], w_eu[0], w_ed[0], MOE_TILE)
    y_prompt = final(h_p, hn_p, p_prompt[0].reshape(t_p, -1), ys, pos_p, wk_p.T, tt=256)
    y_sample = final(h_s2, hn_s, p_sample[0].reshape(dec_batch, -1), ys, pos_s, wk_s.T,
                     tt=256)

    k_prompt = k.reshape(1, batch, seq, n_heads, dh)
    v_prompt = v.reshape(1, batch, seq, n_heads, dh)
    conv_prompt = xr.reshape(batch, seq, c)[:, seq - (n_taps - 1):][None]
    h_prompt = h_last.reshape(1, batch, c)
    k_sample = k_s.reshape(1, dec_batch, 1, n_heads, dh)
    v_sample = v_s.reshape(1, dec_batch, 1, n_heads, dh)
    conv_sample = jnp.concatenate([state_conv[0][:, 1:], xr_s[:, None]], axis=1)[None]
    h_sample = h_s.reshape(1, dec_batch, c)

    return (y_prompt.reshape(batch, seq, d), y_sample.reshape(dec_batch, 1, d),
            k_prompt, v_prompt, conv_prompt, h_prompt,
            k_sample, v_sample, conv_sample, h_sample)
```

```python
import functools

import jax
import jax.numpy as jnp
from jax import lax
from jax.experimental import pallas as pl
from jax.experimental.pallas import tpu as pltpu

F32 = jnp.float32
BF16 = jnp.bfloat16
U32 = jnp.uint32
I32 = jnp.int32
SDS = jax.ShapeDtypeStruct

EPS = 1e-6
LRU_C = 8.0
N_GROUPS = 8
TOPK_GROUPS = 4
TOP_K = 8
ROUTED_SCALE = 2.5

VMEM_LIMIT_BYTES = 56 * 1024 * 1024

NT_DIMS = (((1,), (1,)), ((), ()))


def _params(n_grid_axes):
    return pltpu.CompilerParams(
        dimension_semantics=("arbitrary",) * n_grid_axes,
        vmem_limit_bytes=VMEM_LIMIT_BYTES,
    )


def _rms(x, g):
    return x * lax.rsqrt(jnp.mean(x * x, axis=-1, keepdims=True) + EPS) * g


def _softplus(x):
    return jnp.maximum(x, 0.0) + jnp.log(1.0 + jnp.exp(-jnp.abs(x)))


def _silu(x):
    return x * jax.nn.sigmoid(x)


def _dot(a, b):
    return jnp.dot(a, b, preferred_element_type=F32)


def _pack_halves(x):
    half = x.shape[1] // 2
    return pltpu.pack_elementwise([x[:, :half], x[:, half:]], packed_dtype=BF16)


LANES = 128


def _store_token_rows(ref, packed):
    m, w = packed.shape
    ns = w // LANES
    for s in range(ns):
        ref[pl.ds(s, m, stride=ns), :] = packed[:, s * LANES:(s + 1) * LANES]


def _load_token_rows(ref, m):
    ns = ref.shape[0] // m
    return jnp.concatenate([ref[pl.ds(s, m, stride=ns), :] for s in range(ns)], axis=-1)


def _unpack_halves(w):
    unpack = functools.partial(pltpu.unpack_elementwise, w, packed_dtype=BF16,
                               unpacked_dtype=F32)
    return unpack(index=0), unpack(index=1)


def _head_rms(z, gain, n_heads):
    dh = z.shape[1] // n_heads
    return jnp.concatenate(
        [_rms(z[:, h * dh:(h + 1) * dh], gain) for h in range(n_heads)], axis=-1)


def _in_proj_kernel(x_ref, g_ref, w_ref, qg_ref, kg_ref,
                    xr_ref, gy_ref, q_ref, k_ref, v_ref, kb_ref, vb_ref, *, n_heads):
    c = xr_ref.shape[1]
    dh = c // n_heads
    xn = _rms(x_ref[...], g_ref[...]).astype(BF16)
    z = lambda j: _dot(xn, w_ref[:, j * c:(j + 1) * c])
    xr_ref[...] = z(0)
    gy_ref[...] = jax.nn.gelu(z(1)).astype(BF16)
    q_ref[...] = (_head_rms(z(2), qg_ref[...], n_heads) * dh ** -0.5).astype(BF16)
    k = _head_rms(z(3), kg_ref[...], n_heads)
    k_ref[...] = k
    kb_ref[...] = k.astype(BF16)
    v = z(4)
    v_ref[...] = v
    vb_ref[...] = v.astype(BF16)


def _in_proj(x, g_mix, w_in_bf, q_gain, k_gain, n_heads, tm):
    t, d = x.shape
    c = w_in_bf.shape[1] // 5
    dh = c // n_heads
    tm = min(tm, t)
    assert t % tm == 0
    row = lambda i: (i, 0)
    const = lambda i: (0, 0)
    out_blk = pl.BlockSpec((tm, c), row)
    return pl.pallas_call(
        functools.partial(_in_proj_kernel, n_heads=n_heads),
        grid=(t // tm,),
        in_specs=[
            pl.BlockSpec((tm, d), row),
            pl.BlockSpec((1, d), const),
            pl.BlockSpec(w_in_bf.shape, const, pipeline_mode=pl.Buffered(1)),
            pl.BlockSpec((1, dh), const),
            pl.BlockSpec((1, dh), const),
        ],
        out_specs=[out_blk] * 7,
        out_shape=[SDS((t, c), F32), SDS((t, c), BF16), SDS((t, c), BF16),
                   SDS((t, c), F32), SDS((t, c), F32), SDS((t, c), BF16),
                   SDS((t, c), BF16)],
        compiler_params=_params(1),
        name="in_proj",
    )(x, g_mix, w_in_bf, q_gain, k_gain)


def _lru_gates(xc, wai_ref, ba, bi, lam):
    n_blocks, blk, _ = wai_ref.shape
    xcb = xc.astype(BF16)
    ga, gi = [], []
    for n in range(n_blocks):
        g = _dot(xcb[:, n * blk:(n + 1) * blk], wai_ref[n])
        ga.append(g[:, :blk])
        gi.append(g[:, blk:])
    r = jax.nn.sigmoid(jnp.concatenate(ga, axis=-1) + ba)
    ig = jax.nn.sigmoid(jnp.concatenate(gi, axis=-1) + bi)
    log_a = -LRU_C * r * _softplus(-lam)
    a = jnp.exp(log_a)
    th = jnp.tanh(log_a)
    u = jnp.sqrt(-2.0 * th / (1.0 - th)) * ig * xc
    return a, u


def _lru_prompt_kernel(xr_ref, gy_ref, st_ref, h0_ref, cw_ref, cb_ref, wai_ref,
                       ba_ref, bi_ref, lam_ref, gl_ref, mix_ref, ht_ref,
                       xext_ref, a_ref, u_ref, h_ref, *, tl):
    l = pl.program_id(1)

    @pl.when(l == 0)
    def _():
        xext_ref[0:8, :] = st_ref[0]
        h_ref[...] = h0_ref[0]

    xext_ref[8:8 + tl, :] = xr_ref[0]
    cw = cw_ref[...]
    n_taps = cw.shape[0]
    xc = cb_ref[...]
    for j in range(n_taps):
        off = 8 - (n_taps - 1) + j
        xc = xc + cw[j:j + 1, :] * xext_ref[off:off + tl, :]
    xext_ref[0:8, :] = xext_ref[tl:tl + 8, :]

    a, u = _lru_gates(xc, wai_ref, ba_ref[...], bi_ref[...], lam_ref[...])
    a_ref[...] = a
    u_ref[...] = u

    def step(t, h):
        h = a_ref[pl.ds(t, 1), :] * h + u_ref[pl.ds(t, 1), :]
        u_ref[pl.ds(t, 1), :] = h
        return h

    h = lax.fori_loop(0, tl, step, h_ref[...], unroll=8)
    h_ref[...] = h
    mix_ref[0] = _rms(u_ref[...] * gy_ref[0], gl_ref[...]).astype(BF16)

    @pl.when(l == pl.num_programs(1) - 1)
    def _():
        ht_ref[0] = h


def _lru_prompt(xr, gy, state8, h0, conv_w, conv_b, wai, b_a, b_i, lam, g_lru, tl):
    b, l, c = xr.shape
    tl = min(tl, l)
    assert l % tl == 0 and tl % 8 == 0
    tile = pl.BlockSpec((1, tl, c), lambda i, j: (i, j, 0))
    per_b = lambda rows: pl.BlockSpec((1, rows, c), lambda i, j: (i, 0, 0))
    vec = pl.BlockSpec((1, c), lambda i, j: (0, 0))
    return pl.pallas_call(
        functools.partial(_lru_prompt_kernel, tl=tl),
        grid=(b, l // tl),
        in_specs=[tile, tile, per_b(8), per_b(1),
                  pl.BlockSpec(conv_w.shape, lambda i, j: (0, 0)), vec,
                  pl.BlockSpec(wai.shape, lambda i, j: (0, 0, 0)),
                  vec, vec, vec, vec],
        out_specs=[tile, per_b(1)],
        out_shape=[SDS((b, l, c), BF16), SDS((b, 1, c), F32)],
        scratch_shapes=[pltpu.VMEM((tl + 8, c), F32), pltpu.VMEM((tl, c), F32),
                        pltpu.VMEM((tl, c), F32), pltpu.VMEM((1, c), F32)],
        compiler_params=_params(2),
        name="lru_prompt",
    )(xr, gy, state8, h0, conv_w, conv_b, wai, b_a, b_i, lam, g_lru)


def _lru_step_kernel(xr_ref, gy_ref, st_ref, h0_ref, cw_ref, cb_ref, wai_ref,
                     ba_ref, bi_ref, lam_ref, gl_ref, mix_ref, hn_ref):
    cw = cw_ref[...]
    n_taps = cw.shape[0]
    xc = cb_ref[...] + cw[n_taps - 1:n_taps, :] * xr_ref[...]
    for j in range(n_taps - 1):
        xc = xc + cw[j:j + 1, :] * st_ref[j]
    a, u = _lru_gates(xc, wai_ref, ba_ref[...], bi_ref[...], lam_ref[...])
    h = a * h0_ref[...] + u
    hn_ref[...] = h
    mix_ref[...] = _rms(h * gy_ref[...], gl_ref[...]).astype(BF16)


def _lru_step(xr, gy, state, h0, conv_w, conv_b, wai, b_a, b_i, lam, g_lru):
    b, c = xr.shape
    return pl.pallas_call(
        _lru_step_kernel,
        out_shape=[SDS((b, c), BF16), SDS((b, c), F32)],
        compiler_params=pltpu.CompilerParams(vmem_limit_bytes=VMEM_LIMIT_BYTES),
        name="lru_step",
    )(xr, gy, state, h0, conv_w, conv_b, wai, b_a, b_i, lam, g_lru)


def _attn_prompt_kernel(bias_ref, q_ref, k_ref, v_ref, tri_ref, o_ref, *, tq, tk, hp):
    hg = pl.program_id(1)
    i = pl.program_id(2)
    tri = tri_ref[...]
    dh = q_ref.shape[1] // hp
    r = tq // tk
    qs = [q_ref[:, a * dh:(a + 1) * dh] for a in range(hp)]
    biases = [bias_ref[hg * hp + a] for a in range(hp)]
    row = lax.broadcasted_iota(I32, (tq, tk), 0)
    col = lax.broadcasted_iota(I32, (tq, tk), 1)

    def tile(j, carry, diag):
        start = pl.multiple_of(j * tk, tk)
        new = []
        for a in range(hp):
            c, acc = carry[a]
            kj = k_ref[pl.ds(start, tk), a * dh:(a + 1) * dh]
            vj = v_ref[pl.ds(start, tk), a * dh:(a + 1) * dh]
            z = lax.dot_general(qs[a], kj, NT_DIMS, preferred_element_type=F32) + biases[a]
            sp = _softplus(z)
            if diag is not None:
                causal = col + diag * tk < row
                sp = jnp.where(causal, sp, 0.0)
            drop = _dot(sp.astype(BF16), tri)
            w = jnp.exp(z - c - drop)
            if diag is not None:
                w = jnp.where(causal, w, 0.0)
            acc = acc + _dot(w.astype(BF16), vj)
            c = c + drop[:, 0:1]
            new.append((c, acc))
        return tuple(new)

    carry = tuple((jnp.zeros((tq, 1), F32), jnp.zeros((tq, dh), F32)) for _ in range(hp))
    for dj in reversed(range(r)):
        carry = tile(i * r + dj, carry, dj)
    carry = lax.fori_loop(0, i * r, lambda jj, cr: tile(i * r - 1 - jj, cr, None), carry)
    o_ref[...] = jnp.concatenate([carry[a][1] for a in range(hp)], axis=-1)


def _attn_prompt(q, k, v, sb_bias, batch, n_heads, tq, tk, hp):
    t, c = q.shape
    l = t // batch
    dh = c // n_heads
    tq = min(tq, l)
    tk = min(tk, tq)
    hp = min(hp, n_heads)
    assert l % tq == 0 and tq % tk == 0 and n_heads % hp == 0
    nq = l // tq
    tri = (jnp.arange(tk)[:, None] >= jnp.arange(tk)[None, :]).astype(BF16)
    kv_spec = pl.BlockSpec((l, hp * dh), lambda b, h, i: (b, h))
    q_spec = pl.BlockSpec((tq, hp * dh), lambda b, h, i: (b * nq + i, h))
    return pl.pallas_call(
        functools.partial(_attn_prompt_kernel, tq=tq, tk=tk, hp=hp),
        grid=(batch, n_heads // hp, nq),
        in_specs=[pl.BlockSpec(memory_space=pltpu.SMEM), q_spec, kv_spec, kv_spec,
                  pl.BlockSpec((tk, tk), lambda b, h, i: (0, 0))],
        out_specs=q_spec,
        out_shape=SDS((t, c), F32),
        compiler_params=_params(3),
        name="attn_prompt",
    )(sb_bias, q, k, v, tri)


def _shift_lanes_left(x, s):
    n = x.shape[1]
    col = lax.broadcasted_iota(I32, x.shape, 1)
    return jnp.where(col < n - s, pltpu.roll(x, n - s, axis=1), 0.0)


def _shift_rows_down(x, s):
    rowi = lax.broadcasted_iota(I32, x.shape, 0)
    return jnp.where(rowi >= s, pltpu.roll(x, s, axis=0), 0.0)


def _attn_sample_kernel(pt_ref, bias_ref, q_ref, *refs, n_pages_step):
    del pt_ref
    g_ = n_pages_step
    k_refs = refs[:g_]
    v_refs = refs[g_:2 * g_]
    o_ref, c_ref, acc_ref = refs[2 * g_:]
    step = pl.program_id(1)

    @pl.when(step == 0)
    def _():
        c_ref[...] = jnp.zeros_like(c_ref)
        acc_ref[...] = jnp.zeros_like(acc_ref)

    q = q_ref[0]
    n_heads, dh = q.shape
    page = k_refs[0].shape[0]
    n_col = page * n_heads
    row = lax.broadcasted_iota(I32, (n_heads, n_col), 0)
    col = lax.broadcasted_iota(I32, (n_heads, n_col), 1)
    own = lax.rem(col, n_heads) == row
    pick = lambda t: jnp.sum(jnp.where(own, t, 0.0), axis=0, keepdims=True)
    bias_row = pick(jnp.broadcast_to(bias_ref[...], (n_heads, n_col)))
    prow = lax.broadcasted_iota(I32, (g_, n_col), 0)

    z = jnp.zeros((g_, n_col), F32)
    for g in range(g_):
        kp = k_refs[g][...].reshape(n_col, dh).astype(BF16)
        z8 = lax.dot_general(q, kp, NT_DIMS, preferred_element_type=F32)
        z = jnp.where(prow == g, jnp.broadcast_to(pick(z8), (g_, n_col)), z)
    z = z + bias_row
    sp = _softplus(z)
    log_keep = -sp
    log_after = _shift_lanes_left(log_keep, n_heads)
    total = log_keep
    s = n_heads
    while s < n_col:
        log_after = log_after + _shift_lanes_left(log_after, s)
        total = total + pltpu.roll(total, s, axis=1)
        s *= 2
    before = _shift_rows_down(total, 1)
    s = 1
    while s < g_:
        before = before + _shift_rows_down(before, s)
        s *= 2
    c_in = c_ref[...]
    w = jnp.exp(z - sp + log_after + before + c_in)
    c_ref[...] = c_in + jnp.sum(total, axis=0, keepdims=True)

    acc = acc_ref[...]
    for g in range(g_):
        w8 = jnp.where(own, jnp.broadcast_to(w[g:g + 1, :], (n_heads, n_col)), 0.0)
        vp = v_refs[g][...].reshape(n_col, dh).astype(BF16)
        acc = acc + _dot(w8.astype(BF16), vp)
    acc_ref[...] = acc

    @pl.when(step == pl.num_programs(1) - 1)
    def _():
        o_ref[0] = acc


def _attn_sample(q, cache_k, cache_v, page_table, sb_bias, n_pages_step):
    b, n_heads, dh = q.shape
    _, page, _, _ = cache_k.shape
    n_pages = page_table.shape[1]
    g_ = min(n_pages_step, n_pages)
    assert n_pages % g_ == 0 and page & (page - 1) == 0

    def page_spec(g):
        return pl.BlockSpec(
            (None, page, n_heads, dh),
            lambda i, j, pt: (pt[i, n_pages - 1 - (j * g_ + g)], 0, 0, 0))

    q_spec = pl.BlockSpec((1, n_heads, dh), lambda i, j, pt: (i, 0, 0))
    grid_spec = pltpu.PrefetchScalarGridSpec(
        num_scalar_prefetch=1,
        grid=(b, n_pages // g_),
        in_specs=[pl.BlockSpec((n_heads, 1), lambda i, j, pt: (0, 0)), q_spec]
        + [page_spec(g) for g in range(g_)] * 2,
        out_specs=q_spec,
        scratch_shapes=[pltpu.VMEM((1, page * n_heads), F32),
                        pltpu.VMEM((n_heads, dh), F32)],
    )
    return pl.pallas_call(
        functools.partial(_attn_sample_kernel, n_pages_step=g_),
        grid_spec=grid_spec,
        out_shape=SDS((b, n_heads, dh), F32),
        compiler_params=_params(2),
        name="attn_sample",
    )(page_table, sb_bias.reshape(n_heads, 1), q,
      *([cache_k] * g_), *([cache_v] * g_))


def _out_proj_kernel(ml_ref, att_ref, x_ref, ga_ref, wo_ref, gf_ref, wr_ref,
                     h_ref, hn_ref, lg_ref):
    c = ml_ref.shape[1]
    attn = _rms(att_ref[...], ga_ref[...]).astype(BF16)
    mix = _dot(ml_ref[...], wo_ref[0:c, :]) + _dot(attn, wo_ref[c:2 * c, :])
    h = x_ref[...] + mix
    h_ref[...] = h
    hn = _rms(h, gf_ref[...])
    _store_token_rows(hn_ref, _pack_halves(hn))
    e = lg_ref.shape[0]
    hn_hi = hn.astype(BF16)
    hn_lo = (hn - hn_hi.astype(F32)).astype(BF16)
    a = lax.dot_general(wr_ref[...], hn_hi, NT_DIMS, preferred_element_type=F32)
    b = lax.dot_general(wr_ref[0:e, :], hn_lo, NT_DIMS, preferred_element_type=F32)
    lg_ref[...] = a[0:e, :] + a[e:2 * e, :] + b


def _out_proj(mix_lru, att, x, g_att, w_o_bf, g_ffn, w_router_t, tm):
    t, d = x.shape
    c = mix_lru.shape[1]
    e = w_router_t.shape[0] // 2
    ns = d // 2 // LANES
    tm = min(tm, t)
    assert t % tm == 0
    row = lambda w: pl.BlockSpec((tm, w), lambda i: (i, 0))
    full = lambda a: pl.BlockSpec(a.shape, lambda i: (0, 0), pipeline_mode=pl.Buffered(1))
    return pl.pallas_call(
        _out_proj_kernel,
        grid=(t // tm,),
        in_specs=[row(c), row(c), row(d), full(g_att), full(w_o_bf), full(g_ffn),
                  full(w_router_t)],
        out_specs=[row(d), pl.BlockSpec((tm * ns, LANES), lambda i: (i, 0)),
                   pl.BlockSpec((e, tm), lambda i: (0, i))],
        out_shape=[SDS((t, d), F32), SDS((t * ns, LANES), U32), SDS((e, t), F32)],
        compiler_params=_params(1),
        name="out_proj",
    )(mix_lru, att, x, g_att, w_o_bf, g_ffn, w_router_t)


def _take_top(cur, idx, sentinel, n):
    picked = jnp.zeros(cur.shape, jnp.bool_)
    for _ in range(n):
        m = jnp.max(cur, axis=0, keepdims=True)
        first = jnp.min(jnp.where(cur == m, idx, sentinel), axis=0, keepdims=True)
        pick = idx == first
        picked = jnp.logical_or(picked, pick)
        cur = jnp.where(pick, -jnp.inf, cur)
    return picked


def _router_kernel(lg_ref, eb_ref, cin_ref, ek_ref, rk_ref, wk_ref, cnt_ref, run_ref):
    e, tm = lg_ref.shape
    per = e // N_GROUPS

    @pl.when(pl.program_id(0) == 0)
    def _():
        run_ref[...] = cin_ref[...]

    scores = jax.nn.sigmoid(lg_ref[...])
    biased = scores + eb_ref[...]
    sub = lax.broadcasted_iota(I32, (per, tm), 0)
    group_scores = []
    for g in range(N_GROUPS):
        bg = biased[g * per:(g + 1) * per, :]
        m1 = jnp.max(bg, axis=0, keepdims=True)
        first = jnp.min(jnp.where(bg == m1, sub, per), axis=0, keepdims=True)
        m2 = jnp.max(jnp.where(sub == first, -jnp.inf, bg), axis=0, keepdims=True)
        group_scores.append(m1 + m2)
    gs = jnp.concatenate(group_scores, axis=0)
    gidx = lax.broadcasted_iota(I32, gs.shape, 0)
    gsel = _take_top(gs, gidx, N_GROUPS, TOPK_GROUPS)
    emask = jnp.concatenate(
        [jnp.broadcast_to(gsel[g:g + 1, :], (per, tm)) for g in range(N_GROUPS)],
        axis=0)
    eidx = lax.broadcasted_iota(I32, (e, tm), 0)
    sel = _take_top(jnp.where(emask, biased, -jnp.inf), eidx, e, TOP_K)
    chosen = jnp.where(sel, scores, 0.0)
    gate = chosen / jnp.sum(chosen, axis=0, keepdims=True) * ROUTED_SCALE

    picked = jnp.where(sel, 1.0, 0.0)
    er = lax.broadcasted_iota(I32, (e, e), 0)
    ec = lax.broadcasted_iota(I32, (e, e), 1)
    slot = _dot(jnp.where(ec < er, 1.0, 0.0), picked)
    tr = lax.broadcasted_iota(I32, (tm, tm), 0)
    tc = lax.broadcasted_iota(I32, (tm, tm), 1)
    rank = _dot(picked, jnp.where(tr < tc, 1.0, 0.0)) + run_ref[...]
    run_ref[...] = run_ref[...] + jnp.sum(picked, axis=1, keepdims=True)
    cnt_ref[...] = run_ref[...]

    eidx_f = eidx.astype(F32)
    col = lambda m, v: jnp.sum(jnp.where(m, v, 0.0), axis=0, keepdims=True)
    eks, rks, wks = [], [], []
    for k in range(TOP_K):
        m = jnp.logical_and(sel, slot == k)
        eks.append(col(m, eidx_f))
        rks.append(col(m, rank))
        wks.append(col(m, gate))
    ek_ref[...] = jnp.concatenate(eks, axis=0).astype(I32)
    rk_ref[...] = jnp.concatenate(rks, axis=0).astype(I32)
    wk_ref[...] = jnp.concatenate(wks, axis=0)


def _router(logits_t, e_bias, counts_in, tm):
    e, t = logits_t.shape
    tm = min(tm, t)
    assert t % tm == 0
    blk = pl.BlockSpec((e, tm), lambda i: (0, i))
    vec = pl.BlockSpec((e, 1), lambda i: (0, 0))
    slot_blk = pl.BlockSpec((TOP_K, tm), lambda i: (0, i))
    return pl.pallas_call(
        _router_kernel,
        grid=(t // tm,),
        in_specs=[blk, vec, vec],
        out_specs=[slot_blk, slot_blk, slot_blk, vec],
        out_shape=[SDS((TOP_K, t), I32), SDS((TOP_K, t), I32), SDS((TOP_K, t), F32),
                   SDS((e, 1), F32)],
        scratch_shapes=[pltpu.VMEM((e, 1), F32)],
        compiler_params=_params(1),
        name="router",
    )(logits_t, e_bias.reshape(e, 1), counts_in)


def _positions_kernel(off_ref, ek_ref, rk_ref, pos_ref, *, ns):
    ek = ek_ref[...]
    pos = rk_ref[...]
    for e in range(off_ref.shape[0]):
        pos = pos + jnp.where(ek == e, off_ref[e], 0)
    pos_ref[...] = pos * ns


def _positions(offsets, ek, rk, ns, tm):
    k, t = ek.shape
    tm = min(tm, t)
    assert t % tm == 0
    blk = pl.BlockSpec((k, tm), lambda i: (0, i))
    return pl.pallas_call(
        functools.partial(_positions_kernel, ns=ns),
        grid=(t // tm,),
        in_specs=[pl.BlockSpec(memory_space=pltpu.SMEM), blk, blk],
        out_specs=blk,
        out_shape=SDS((k, t), I32),
        compiler_params=_params(1),
        name="positions",
    )(offsets, ek, rk)


def _work_list(counts, n_tiles, tm):
    e = counts.shape[0]
    ends = jnp.cumsum(counts)
    offsets = ends - counts
    n_work = n_tiles + e
    first_tile = offsets // tm
    n_e = jnp.where(counts > 0, (ends - 1) // tm - first_tile + 1, 0)
    w_end = jnp.cumsum(n_e)
    w_start = w_end - n_e
    total = w_end[-1]
    w = jnp.arange(n_work, dtype=I32)
    valid = w < total
    wc = jnp.minimum(w, total - 1)
    ex = jnp.minimum(jnp.sum((w_end[None, :] <= wc[:, None]).astype(I32), axis=1), e - 1)
    onehot = ex[:, None] == jnp.arange(e, dtype=I32)[None, :]
    take = lambda v: jnp.sum(jnp.where(onehot, v[None, :], 0), axis=1)
    tile = take(first_tile) + (wc - take(w_start))
    lo = jnp.where(valid, jnp.maximum(take(offsets), tile * tm) - tile * tm, 0)
    hi = jnp.where(valid, jnp.minimum(take(ends), (tile + 1) * tm) - tile * tm, 0)
    prev_tile = jnp.concatenate([jnp.full((1,), -1, I32), tile[:-1]])
    next_tile = jnp.concatenate([tile[1:], jnp.full((1,), -1, I32)])
    prev_ex = jnp.concatenate([jnp.full((1,), -1, I32), ex[:-1]])
    first = valid & (tile != prev_tile)
    last = valid & ((tile != next_tile) | (w == total - 1))
    new_ex = valid & (ex != prev_ex)
    flags = first.astype(I32) + 2 * last.astype(I32) + 4 * new_ex.astype(I32)
    return (tile.astype(I32), ex, lo.astype(I32), hi.astype(I32), flags), offsets


def _dispatch_kernel(pos_ref, xa_ref, xb_ref, xs_ref, stage_ref, sem_ref, *pad, k, ns):
    tt = xa_ref.shape[0] // ns
    tb = xb_ref.shape[0] // ns
    s = pl.program_id(0)
    n = pl.num_programs(0)
    slot = lax.rem(s, 2)

    if pad:
        zero_ref, zsem_ref = pad
        n_pad = zero_ref.shape[0]
        fill = pltpu.make_async_copy(
            zero_ref, xs_ref.at[pl.ds(xs_ref.shape[0] - n_pad, n_pad)], zsem_ref.at[0])

        @pl.when(s == 0)
        def _():
            zero_ref[...] = jnp.zeros_like(zero_ref)
            fill.start()

        @pl.when(s == n - 1)
        def _():
            fill.wait()

    def wait_slot(sl, tokens):
        rows = tokens * ns
        for _ in range(k):
            pltpu.make_async_copy(stage_ref.at[sl, pl.ds(0, rows)], xs_ref.at[pl.ds(0, rows)],
                                  sem_ref.at[sl]).wait()

    def issue(tokens):
        def body(r, carry):
            src = stage_ref.at[slot, pl.ds(pl.multiple_of(r * ns, ns), ns)]
            for kk in range(k):
                dst = pl.multiple_of(pos_ref[0, 0, r * k + kk], ns)
                pltpu.make_async_copy(src, xs_ref.at[pl.ds(dst, ns)],
                                      sem_ref.at[slot]).start(priority=kk % 2)
            return carry
        lax.fori_loop(0, tokens, body, 0)

    @pl.when(s >= 2)
    def _():
        wait_slot(slot, tt)

    @pl.when(s < n - 1)
    def _():
        stage_ref[slot] = xa_ref[...]
        issue(tt)

    @pl.when(s == n - 1)
    def _():
        stage_ref[slot, 0:tb * ns, :] = xb_ref[...]
        issue(tb)
        wait_slot(slot, tb)
        wait_slot(1 - slot, tt)


def _dispatch(xa, xb, pos_tk, n_slots, ns, tt):
    ta = xa.shape[0] // ns
    tb = xb.shape[0] // ns
    k = pos_tk.shape[1]
    tt = min(tt, ta)
    assert ta % tt == 0 and tb <= tt
    n_a = ta // tt
    pos_b = jnp.zeros((tt * k,), I32).at[:tb * k].set(pos_tk[ta:].reshape(-1))
    pos = jnp.concatenate([pos_tk[:ta].reshape(-1), pos_b]).reshape(n_a + 1, 1, tt * k)
    n_pad = n_slots - (ta + tb) * k
    scratch = [pltpu.VMEM((2, tt * ns, LANES), U32), pltpu.SemaphoreType.DMA((2,))]
    if n_pad:
        scratch += [pltpu.VMEM((n_pad * ns, LANES), U32), pltpu.SemaphoreType.DMA((1,))]
    return pl.pallas_call(
        functools.partial(_dispatch_kernel, k=k, ns=ns),
        grid=(n_a + 1,),
        in_specs=[pl.BlockSpec((1, 1, tt * k), lambda s: (s, 0, 0), memory_space=pltpu.SMEM),
                  pl.BlockSpec((tt * ns, LANES), lambda s: (jnp.minimum(s, n_a - 1), 0)),
                  pl.BlockSpec((tb * ns, LANES), lambda s: (0, 0))],
        out_specs=pl.BlockSpec(memory_space=pl.ANY),
        out_shape=SDS((n_slots * ns, LANES), U32),
        scratch_shapes=scratch,
        compiler_params=_params(1),
        name="dispatch",
    )(pos, xa, xb)


def _moe_kernel(tile_ref, ex_ref, lo_ref, hi_ref, flag_ref,
                xs_ref, wg_ref, wu_ref, wd_ref, ys_ref,
                wgb_ref, wub_ref, wdb_ref, acc_ref):
    del tile_ref, ex_ref
    w = pl.program_id(0)
    lo = lo_ref[w]
    hi = hi_ref[w]
    flags = flag_ref[w]

    @pl.when((flags & 4) != 0)
    def _():
        wgb_ref[...] = wg_ref[...].astype(BF16)
        wub_ref[...] = wu_ref[...].astype(BF16)
        wdb_ref[...] = wd_ref[...].astype(BF16)

    @pl.when(hi > lo)
    def _():
        x_lo, x_hi = _unpack_halves(_load_token_rows(xs_ref, acc_ref.shape[0]))
        x_lo = x_lo.astype(BF16)
        x_hi = x_hi.astype(BF16)
        half = x_lo.shape[1]
        hg = _dot(x_lo, wgb_ref[0:half, :]) + _dot(x_hi, wgb_ref[half:2 * half, :])
        hu = _dot(x_lo, wub_ref[0:half, :]) + _dot(x_hi, wub_ref[half:2 * half, :])
        rowi = lax.broadcasted_iota(I32, hg.shape, 0)
        mine = jnp.logical_and(rowi >= lo, rowi < hi)
        act = jnp.where(mine, _silu(hg) * hu, 0.0)
        y = _dot(act.astype(BF16), wdb_ref[...])

        @pl.when((flags & 3) == 3)
        def _():
            _store_token_rows(ys_ref, _pack_halves(y))

        @pl.when((flags & 3) == 1)
        def _():
            acc_ref[...] = y

        @pl.when((flags & 1) == 0)
        def _():
            acc_ref[...] += y

    @pl.when((flags & 3) == 2)
    def _():
        _store_token_rows(ys_ref, _pack_halves(acc_ref[...]))


def _moe(xs, work, w_eg, w_eu, w_ed, tm):
    e, d, f = w_eg.shape
    ns = d // 2 // LANES
    n_work = work[0].shape[0]
    row_spec = pl.BlockSpec((tm * ns, LANES), lambda w, tile, ex, lo, hi, fl: (tile[w], 0))
    up_spec = pl.BlockSpec((None, d, f), lambda w, tile, ex, lo, hi, fl: (ex[w], 0, 0))
    dn_spec = pl.BlockSpec((None, f, d), lambda w, tile, ex, lo, hi, fl: (ex[w], 0, 0))
    grid_spec = pltpu.PrefetchScalarGridSpec(
        num_scalar_prefetch=5,
        grid=(n_work,),
        in_specs=[row_spec, up_spec, up_spec, dn_spec],
        out_specs=row_spec,
        scratch_shapes=[pltpu.VMEM((d, f), BF16), pltpu.VMEM((d, f), BF16),
                        pltpu.VMEM((f, d), BF16), pltpu.VMEM((tm, d), F32)],
    )
    return pl.pallas_call(
        _moe_kernel,
        grid_spec=grid_spec,
        out_shape=SDS(xs.shape, U32),
        compiler_params=_params(1),
        name="moe",
    )(*work, xs, w_eg, w_eu, w_ed)


def _final_kernel(pos_ref, posn_ref, w_ref, h_ref, hn_ref, p_ref, ys_ref,
                  wsg_ref, wsu_ref, wsd_ref, gp_ref, wpg_ref, wpp_ref, y_ref,
                  gbuf_ref, sem_ref, *, tt, k):
    s = pl.program_id(0)
    n = pl.num_programs(0)
    slot = lax.rem(s, 2)

    ns = hn_ref.shape[0] // tt

    def issue(idx_ref, sl):
        def body(r, carry):
            row0 = pl.multiple_of(r * ns, ns)
            for kk in range(k):
                src = pl.multiple_of(idx_ref[0, 0, r * k + kk], ns)
                pltpu.make_async_copy(ys_ref.at[pl.ds(src, ns)],
                                      gbuf_ref.at[sl, kk, pl.ds(row0, ns)],
                                      sem_ref.at[sl]).start(priority=kk % 2)
            return carry
        lax.fori_loop(0, tt, body, 0)

    @pl.when(s == 0)
    def _():
        issue(pos_ref, 0)

    @pl.when(s + 1 < n)
    def _():
        issue(posn_ref, 1 - slot)

    for kk in range(k):
        pltpu.make_async_copy(ys_ref.at[pl.ds(0, tt * ns)], gbuf_ref.at[slot, kk],
                              sem_ref.at[slot]).wait()

    wts = w_ref[...]
    r_lo = None
    for kk in range(k):
        lo, hi = _unpack_halves(_load_token_rows(gbuf_ref.at[slot, kk], tt))
        wk = wts[:, kk:kk + 1]
        r_lo = wk * lo if r_lo is None else r_lo + wk * lo
        r_hi = wk * hi if kk == 0 else r_hi + wk * hi
    routed = jnp.concatenate([r_lo, r_hi], axis=-1)

    n_lo, n_hi = _unpack_halves(_load_token_rows(hn_ref, tt))
    n_lo = n_lo.astype(BF16)
    n_hi = n_hi.astype(BF16)
    half = n_lo.shape[1]
    sg = _dot(n_lo, wsg_ref[0:half, :]) + _dot(n_hi, wsg_ref[half:2 * half, :])
    su = _dot(n_lo, wsu_ref[0:half, :]) + _dot(n_hi, wsu_ref[half:2 * half, :])
    shared = _dot((_silu(sg) * su).astype(BF16), wsd_ref[...])
    h2 = h_ref[...] + routed + shared
    gate = jax.nn.sigmoid(_dot(_rms(h2, gp_ref[...]).astype(BF16), wpg_ref[...]))
    y_ref[...] = h2 + gate * _dot(p_ref[...].astype(BF16), wpp_ref[...])


def _final(h, hn_packed, p, ys, pos_tk, w_tk, w_sg, w_su, w_sd, g_ple, w_pg, w_pp, tt):
    t, d = h.shape
    k = pos_tk.shape[1]
    ns = d // 2 // LANES
    tt = min(tt, t)
    assert t % tt == 0
    n = t // tt
    pos = pos_tk.reshape(n, 1, tt * k)
    row = lambda w: pl.BlockSpec((tt, w), lambda s: (s, 0))
    const = lambda a: pl.BlockSpec(a.shape, lambda s: (0, 0), pipeline_mode=pl.Buffered(1))
    smem = lambda fn: pl.BlockSpec((1, 1, tt * k), fn, memory_space=pltpu.SMEM)
    return pl.pallas_call(
        functools.partial(_final_kernel, tt=tt, k=k),
        grid=(n,),
        in_specs=[smem(lambda s: (s, 0, 0)),
                  smem(lambda s: (jnp.minimum(s + 1, n - 1), 0, 0)),
                  row(k), row(d), pl.BlockSpec((tt * ns, LANES), lambda s: (s, 0)),
                  row(p.shape[1]),
                  pl.BlockSpec(memory_space=pl.ANY),
                  const(w_sg), const(w_su), const(w_sd), const(g_ple), const(w_pg),
                  const(w_pp)],
        out_specs=row(d),
        out_shape=SDS((t, d), F32),
        scratch_shapes=[pltpu.VMEM((2, k, tt * ns, LANES), U32),
                        pltpu.SemaphoreType.DMA((2,))],
        compiler_params=_params(1),
        name="final",
    )(pos, pos, w_tk, h, hn_packed, p, ys, w_sg, w_su, w_sd, g_ple, w_pg, w_pp)


MOE_TILE = 512


def kernel(x_prompt, x_sample, p_prompt, p_sample, cache_k, cache_v, state_conv, state_h, page_table, g_mix, w_in, conv_w, conv_b, w_a, b_a, w_i, b_i, lru_lambda, q_gain, k_gain, sb_bias, g_lru_out, g_att_out, w_o, g_ffn, w_router, e_bias, w_eg, w_eu, w_ed, w_sg, w_su, w_sd, g_ple, w_ple_gate, w_ple_proj):
    depth = g_mix.shape[0]
    assert depth == 1
    batch, seq, d = x_prompt.shape
    dec_batch, dec_seq, _ = x_sample.shape
    assert dec_seq == 1
    n_heads, dh = cache_k.shape[3], cache_k.shape[4]
    c = n_heads * dh
    n_taps = conv_w.shape[1]
    assert w_in.shape[2] == 5 * c and seq >= n_taps - 1
    t_p = batch * seq

    row = lambda a: a.reshape(1, -1)
    wai = jnp.concatenate([w_a[0], w_i[0]], axis=-1).astype(BF16)
    lru_w = (conv_w[0], row(conv_b[0]), wai, row(b_a[0]), row(b_i[0]),
             row(lru_lambda[0]), row(g_lru_out[0]))
    in_proj = functools.partial(_in_proj, g_mix=row(g_mix[0]), w_in_bf=w_in[0].astype(BF16),
                                q_gain=row(q_gain[0]), k_gain=row(k_gain[0]),
                                n_heads=n_heads)
    wr_t = w_router[0].T
    wr_hi = wr_t.astype(BF16)
    wr_lo = (wr_t - wr_hi.astype(F32)).astype(BF16)
    out_proj = functools.partial(_out_proj, g_att=row(g_att_out[0]), w_o_bf=w_o[0].astype(BF16),
                                 g_ffn=row(g_ffn[0]),
                                 w_router_t=jnp.concatenate([wr_hi, wr_lo], axis=0))
    final = functools.partial(
        _final, w_sg=w_sg[0].astype(BF16), w_su=w_su[0].astype(BF16),
        w_sd=w_sd[0].astype(BF16), g_ple=row(g_ple[0]), w_pg=w_ple_gate[0].astype(BF16),
        w_pp=w_ple_proj[0].astype(BF16))

    xp = x_prompt.reshape(t_p, d)
    xr, gy, q, k, v, kb, vb = in_proj(xp, tm=512)
    mix_lru, h_last = _lru_prompt(
        xr.reshape(batch, seq, c), gy.reshape(batch, seq, c),
        jnp.zeros((batch, 8, c), F32), jnp.zeros((batch, 1, c), F32), *lru_w, tl=256)
    att = _attn_prompt(q, kb, vb, sb_bias[0], batch, n_heads, tq=512, tk=256, hp=4)
    h_p, hn_p, logits_p = out_proj(mix_lru.reshape(t_p, c), att, xp, tm=512)

    xs_ = x_sample.reshape(dec_batch, d)
    xr_s, gy_s, q_s, k_s, v_s, _, _ = in_proj(xs_, tm=dec_batch)
    mix_lru_s, h_s = _lru_step(xr_s, gy_s, jnp.swapaxes(state_conv[0], 0, 1),
                               state_h[0], *lru_w)
    att_s = _attn_sample(q_s.reshape(dec_batch, n_heads, dh), cache_k[0], cache_v[0],
                         page_table, sb_bias[0], n_pages_step=16)
    h_s2, hn_s, logits_s = out_proj(mix_lru_s, att_s.reshape(dec_batch, c), xs_, tm=dec_batch)

    n_experts = w_router.shape[2]
    ek_p, rk_p, wk_p, counts_p = _router(logits_p, e_bias[0],
                                         jnp.zeros((n_experts, 1), F32), tm=256)
    ek_s, rk_s, wk_s, counts = _router(logits_s, e_bias[0], counts_p, tm=256)
    n_tiles = pl.cdiv((t_p + dec_batch) * TOP_K, MOE_TILE)
    work, offsets = _work_list(counts[:, 0].astype(I32), n_tiles, MOE_TILE)
    ns = d // 2 // LANES
    pos_p = _positions(offsets, ek_p, rk_p, ns, tm=512).T
    pos_s = _positions(offsets, ek_s, rk_s, ns, tm=512).T
    rows = _dispatch(hn_p, hn_s, jnp.concatenate([pos_p, pos_s]), n_tiles * MOE_TILE, ns,
                     tt=256)
    ys = _moe(rows, work, w_eg[0], w_eu[0], w_ed[0], MOE_TILE)
    y_prompt = final(h_p, hn_p, p_prompt[0].reshape(t_p, -1), ys, pos_p, wk_p.T, tt=256)
    y_sample = final(h_s2, hn_s, p_sample[0].reshape(dec_batch, -1), ys, pos_s, wk_s.T,
                     tt=256)

    k_prompt = k.reshape(1, batch, seq, n_heads, dh)
    v_prompt = v.reshape(1, batch, seq, n_heads, dh)
    conv_prompt = xr.reshape(batch, seq, c)[:, seq - (n_taps - 1):][None]
    h_prompt = h_last.reshape(1, batch, c)
    k_sample = k_s.reshape(1, dec_batch, 1, n_heads, dh)
    v_sample = v_s.reshape(1, dec_batch, 1, n_heads, dh)
    conv_sample = jnp.concatenate([state_conv[0][:, 1:], xr_s[:, None]], axis=1)[None]
    h_sample = h_s.reshape(1, dec_batch, c)

    return (y_prompt.reshape(batch, seq, d), y_sample.reshape(dec_batch, 1, d),
            k_prompt, v_prompt, conv_prompt, h_prompt,
            k_sample, v_sample, conv_sample, h_sample)
```
